```python
import math
import jax, jax.numpy as jnp
from jax import lax
import numpy as np

D_MODEL = 2048
BATCH = 8
SEQ = 2048
DEPTH = 2

N_MIXERS = 2
RMS_EPS = 1e-6
D_SSM = D_MODEL // 2
SSM_GROUP_CH = 16
N_SSM_GROUPS = D_SSM // SSM_GROUP_CH
SSM_STATE = 64
STEP_MIN = 1e-3
STEP_MAX = 1e-1
FOX_HEAD_DIM = 128
FOX_HEADS = D_MODEL // FOX_HEAD_DIM
Q_BLOCK = 128
FORGET_BIAS_LO = 1.0
FORGET_BIAS_HI = 4.0
N_EXPERT_GROUPS = 8
EXPERTS_PER_GROUP = 8
N_EXPERTS = N_EXPERT_GROUPS * EXPERTS_PER_GROUP
TOP_K = 2
D_EXPERT = D_MODEL // 4
DISPATCH_BLOCK = 256

kernel_name = "hybrid_s5_fox_hier_moe"


def rms_norm(x, gain):
    xf = x.astype(jnp.float32)
    y = xf * lax.rsqrt(jnp.mean(xf * xf, axis=-1, keepdims=True) + RMS_EPS)
    return (y * gain.astype(jnp.float32)).astype(x.dtype)


def _diag_complex_combine(e1, e2):
    a1r, a1i, b1r, b1i = e1
    a2r, a2i, b2r, b2i = e2
    return (a2r * a1r - a2i * a1i,
            a2r * a1i + a2i * a1r,
            a2r * b1r - a2i * b1i + b2r,
            a2r * b1i + a2i * b1r + b2i)


def s5_mixer(h, w_in, lam_re, lam_im, b_re, b_im, c_re, c_im, d_skip, log_step, w_glu, b_glu, w_out):
    f32 = jnp.float32
    bsz, seq, _ = h.shape
    u = h @ w_in
    uf = u.astype(f32).reshape(bsz, seq, N_SSM_GROUPS, SSM_GROUP_CH)
    lr = lam_re.astype(f32)
    li = lam_im.astype(f32)
    step = jnp.exp(log_step.astype(f32))[:, None]
    mag = jnp.exp(lr * step)
    ab_re = mag * jnp.cos(li * step)
    ab_im = mag * jnp.sin(li * step)
    den = lr * lr + li * li
    nr = ab_re - 1.0
    ni = ab_im
    fr = (nr * lr + ni * li) / den
    fi = (ni * lr - nr * li) / den
    br = b_re.astype(f32)
    bi = b_im.astype(f32)
    bb_re = fr[..., None] * br - fi[..., None] * bi
    bb_im = fr[..., None] * bi + fi[..., None] * br
    bu_re = jnp.einsum('bsgc,gnc->sbgn', uf, bb_re)
    bu_im = jnp.einsum('bsgc,gnc->sbgn', uf, bb_im)
    a_re = jnp.broadcast_to(ab_re, (seq, 1) + ab_re.shape)
    a_im = jnp.broadcast_to(ab_im, (seq, 1) + ab_im.shape)
    _, _, x_re, x_im = lax.associative_scan(_diag_complex_combine, (a_re, a_im, bu_re, bu_im), axis=0)
    y = (jnp.einsum('sbgn,gcn->bsgc', x_re, c_re.astype(f32))
         - jnp.einsum('sbgn,gcn->bsgc', x_im, c_im.astype(f32)))
    y = y.reshape(bsz, seq, D_SSM) + d_skip.astype(f32) * u.astype(f32)
    y = y.astype(h.dtype)
    g = jax.nn.gelu(y)
    out = g * jax.nn.sigmoid(g @ w_glu + b_glu)
    return out @ w_out


def fox_mixer(h, w_in, b_forget, w_out):
    f32 = jnp.float32
    bsz, seq, _ = h.shape
    proj = h @ w_in
    q, k, v, gate, f_logit = jnp.split(proj, [D_MODEL, 2 * D_MODEL, 3 * D_MODEL, 4 * D_MODEL], axis=-1)

    def heads(t):
        return t.reshape(bsz, seq, FOX_HEADS, FOX_HEAD_DIM).transpose(0, 2, 1, 3)

    q, k, v = heads(q), heads(k), heads(v)
    log_f = jax.nn.log_sigmoid((f_logit + b_forget).astype(f32))
    cum = jnp.cumsum(log_f, axis=1).transpose(0, 2, 1)
    n_blocks = seq // Q_BLOCK
    q_blocks = q.reshape(bsz, FOX_HEADS, n_blocks, Q_BLOCK, FOX_HEAD_DIM).transpose(2, 0, 1, 3, 4)
    c_blocks = cum.reshape(bsz, FOX_HEADS, n_blocks, Q_BLOCK).transpose(2, 0, 1, 3)
    key_pos = jnp.arange(seq)
    scale = FOX_HEAD_DIM ** -0.5

    def one_block(args):
        qb, cqb, blk = args
        q_pos = blk * Q_BLOCK + jnp.arange(Q_BLOCK)
        logits = jnp.einsum('bhqd,bhkd->bhqk', qb, k).astype(f32) * scale
        logits = logits + (cqb[..., :, None] - cum[:, :, None, :])
        logits = jnp.where(key_pos[None, :] <= q_pos[:, None], logits, -jnp.inf)
        probs = jax.nn.softmax(logits, axis=-1).astype(v.dtype)
        return jnp.einsum('bhqk,bhkd->bhqd', probs, v)

    o = lax.map(one_block, (q_blocks, c_blocks, jnp.arange(n_blocks)))
    o = o.transpose(1, 0, 3, 2, 4).reshape(bsz, seq, D_MODEL)
    o = o * jax.nn.sigmoid(gate)
    return o @ w_out


def hier_moe(h, w_group, b_group, w_expert, b_expert, w_gate, w_up, w_down):
    f32 = jnp.float32
    bsz, seq, d = h.shape
    xt = h.reshape(-1, d)
    n_tok = xt.shape[0]
    group_logits = (xt @ w_group + b_group).astype(f32)
    group_probs = jax.nn.softmax(group_logits, axis=-1)
    g_sel = jnp.argmax(group_logits, axis=-1)
    p_group = jnp.take_along_axis(group_probs, g_sel[:, None], axis=1)[:, 0]
    expert_logits = (xt @ w_expert + b_expert).astype(f32).reshape(n_tok, N_EXPERT_GROUPS, EXPERTS_PER_GROUP)
    in_group = jnp.take_along_axis(expert_logits, g_sel[:, None, None], axis=1)[:, 0]
    top_logits, top_local = lax.top_k(in_group, TOP_K)
    top_w = jax.nn.softmax(top_logits, axis=-1) * p_group[:, None]
    expert_ids = g_sel[:, None] * EXPERTS_PER_GROUP + top_local

    n_pairs = n_tok * TOP_K
    flat_e = expert_ids.reshape(-1).astype(jnp.int32)
    flat_w = top_w.reshape(-1)
    flat_tok = jnp.repeat(jnp.arange(n_tok, dtype=jnp.int32), TOP_K)
    order = jnp.argsort(flat_e)
    se, stok, sw = flat_e[order], flat_tok[order], flat_w[order]
    counts = jnp.zeros((N_EXPERTS,), jnp.int32).at[flat_e].add(1)
    padded = ((counts + DISPATCH_BLOCK - 1) // DISPATCH_BLOCK) * DISPATCH_BLOCK
    start = jnp.cumsum(counts) - counts
    pend = jnp.cumsum(padded)
    pstart = pend - padded
    dest = pstart[se] + (jnp.arange(n_pairs, dtype=jnp.int32) - start[se])
    n_blocks = -(-n_pairs // DISPATCH_BLOCK) + N_EXPERTS
    rows = n_blocks * DISPATCH_BLOCK
    x_disp = jnp.zeros((rows, d), h.dtype).at[dest].set(xt[stok])
    w_disp = jnp.zeros((rows,), f32).at[dest].set(sw)
    tok_disp = jnp.full((rows,), n_tok, jnp.int32).at[dest].set(stok)
    block_start = jnp.arange(n_blocks, dtype=jnp.int32) * DISPATCH_BLOCK
    block_expert = jnp.minimum(jnp.searchsorted(pend, block_start, side='right'), N_EXPERTS - 1)

    def expert_block(args):
        xb, e = args
        hid = jax.nn.silu(xb @ w_gate[e]) * (xb @ w_up[e])
        return hid @ w_down[e]

    y_disp = lax.map(expert_block, (x_disp.reshape(n_blocks, DISPATCH_BLOCK, d), block_expert)).reshape(rows, d)
    y_disp = y_disp * w_disp[:, None].astype(y_disp.dtype)
    out = jnp.zeros((n_tok, d), h.dtype).at[tok_disp].add(y_disp, mode='drop')
    return out.reshape(bsz, seq, d)


def _normal(key, shape, scale):
    return jax.random.normal(key, shape, jnp.float32) * scale


def _gain(key):
    return 1.0 + _normal(key, (D_MODEL,), 0.02)


def _moe_params(key):
    k = jax.random.split(key, 7)
    return (
        _normal(k[0], (D_MODEL, N_EXPERT_GROUPS), D_MODEL ** -0.5),
        _normal(k[1], (N_EXPERT_GROUPS,), 0.01),
        _normal(k[2], (D_MODEL, N_EXPERTS), D_MODEL ** -0.5),
        _normal(k[3], (N_EXPERTS,), 0.01),
        _normal(k[4], (N_EXPERTS, D_MODEL, D_EXPERT), D_MODEL ** -0.5),
        _normal(k[5], (N_EXPERTS, D_MODEL, D_EXPERT), D_MODEL ** -0.5),
        _normal(k[6], (N_EXPERTS, D_EXPERT, D_MODEL), D_EXPERT ** -0.5),
    )


def setup_inputs(seed: int = 0) -> dict:
    key = jax.random.key(seed)
    ks = jax.random.split(key, 24)
    x = jax.random.normal(ks[0], (BATCH, SEQ, D_MODEL), jnp.float32)
    n_idx = jnp.arange(SSM_STATE, dtype=jnp.float32)
    lam_re = -0.5 + _normal(ks[2], (N_SSM_GROUPS, SSM_STATE), 0.01)
    lam_im = math.pi * n_idx[None, :] + _normal(ks[3], (N_SSM_GROUPS, SSM_STATE), 0.01)
    log_step = jax.random.uniform(ks[4], (N_SSM_GROUPS,), jnp.float32, math.log(STEP_MIN), math.log(STEP_MAX))
    b_scale = (2.0 * SSM_GROUP_CH) ** -0.5
    c_scale = SSM_STATE ** -0.5
    m0 = _moe_params(ks[14])
    m1 = _moe_params(ks[15])
    return {
        "x": x,
        "l0_mix_norm": _gain(ks[1]),
        "l0_s5_w_in": _normal(ks[5], (D_MODEL, D_SSM), D_MODEL ** -0.5),
        "l0_s5_lambda_re": lam_re,
        "l0_s5_lambda_im": lam_im,
        "l0_s5_b_re": _normal(ks[6], (N_SSM_GROUPS, SSM_STATE, SSM_GROUP_CH), b_scale),
        "l0_s5_b_im": _normal(ks[7], (N_SSM_GROUPS, SSM_STATE, SSM_GROUP_CH), b_scale),
        "l0_s5_c_re": _normal(ks[8], (N_SSM_GROUPS, SSM_GROUP_CH, SSM_STATE), c_scale),
        "l0_s5_c_im": _normal(ks[9], (N_SSM_GROUPS, SSM_GROUP_CH, SSM_STATE), c_scale),
        "l0_s5_d": _normal(ks[10], (D_SSM,), 1.0),
        "l0_s5_log_step": log_step,
        "l0_s5_w_glu": _normal(ks[11], (D_SSM, D_SSM), D_SSM ** -0.5),
        "l0_s5_b_glu": _normal(ks[12], (D_SSM,), 0.01),
        "l0_s5_w_out": _normal(ks[13], (D_SSM, D_MODEL), D_SSM ** -0.5),
        "l0_ffn_norm": _gain(ks[16]),
        "l0_moe_w_group": m0[0],
        "l0_moe_b_group": m0[1],
        "l0_moe_w_expert": m0[2],
        "l0_moe_b_expert": m0[3],
        "l0_moe_w_gate": m0[4],
        "l0_moe_w_up": m0[5],
        "l0_moe_w_down": m0[6],
        "l1_mix_norm": _gain(ks[17]),
        "l1_fox_w_in": _normal(ks[18], (D_MODEL, 4 * D_MODEL + FOX_HEADS), D_MODEL ** -0.5),
        "l1_fox_b_forget": jax.random.uniform(ks[19], (FOX_HEADS,), jnp.float32, FORGET_BIAS_LO, FORGET_BIAS_HI),
        "l1_fox_w_out": _normal(ks[20], (D_MODEL, D_MODEL), D_MODEL ** -0.5),
        "l1_ffn_norm": _gain(ks[21]),
        "l1_moe_w_group": m1[0],
        "l1_moe_b_group": m1[1],
        "l1_moe_w_expert": m1[2],
        "l1_moe_b_expert": m1[3],
        "l1_moe_w_gate": m1[4],
        "l1_moe_w_up": m1[5],
        "l1_moe_w_down": m1[6],
        "final_norm": _gain(ks[22]),
    }


def reference(x, l0_mix_norm, l0_s5_w_in, l0_s5_lambda_re, l0_s5_lambda_im, l0_s5_b_re, l0_s5_b_im,
              l0_s5_c_re, l0_s5_c_im, l0_s5_d, l0_s5_log_step, l0_s5_w_glu, l0_s5_b_glu, l0_s5_w_out,
              l0_ffn_norm, l0_moe_w_group, l0_moe_b_group, l0_moe_w_expert, l0_moe_b_expert,
              l0_moe_w_gate, l0_moe_w_up, l0_moe_w_down,
              l1_mix_norm, l1_fox_w_in, l1_fox_b_forget, l1_fox_w_out,
              l1_ffn_norm, l1_moe_w_group, l1_moe_b_group, l1_moe_w_expert, l1_moe_b_expert,
              l1_moe_w_gate, l1_moe_w_up, l1_moe_w_down,
              final_norm):
    mix_norms = [l0_mix_norm, l1_mix_norm]
    ffn_norms = [l0_ffn_norm, l1_ffn_norm]
    mixer_params = [
        (l0_s5_w_in, l0_s5_lambda_re, l0_s5_lambda_im, l0_s5_b_re, l0_s5_b_im, l0_s5_c_re, l0_s5_c_im,
         l0_s5_d, l0_s5_log_step, l0_s5_w_glu, l0_s5_b_glu, l0_s5_w_out),
        (l1_fox_w_in, l1_fox_b_forget, l1_fox_w_out),
    ]
    moe_params = [
        (l0_moe_w_group, l0_moe_b_group, l0_moe_w_expert, l0_moe_b_expert, l0_moe_w_gate, l0_moe_w_up, l0_moe_w_down),
        (l1_moe_w_group, l1_moe_b_group, l1_moe_w_expert, l1_moe_b_expert, l1_moe_w_gate, l1_moe_w_up, l1_moe_w_down),
    ]
    h = x
    for i in range(DEPTH):
        hn = rms_norm(h, mix_norms[i])
        if i % N_MIXERS == 0:
            h = h + s5_mixer(hn, *mixer_params[i])
        else:
            h = h + fox_mixer(hn, *mixer_params[i])
        h = h + hier_moe(rms_norm(h, ffn_norms[i]), *moe_params[i])
    return rms_norm(h, final_norm)
```

```python
import functools
import math

import jax
import jax.numpy as jnp
from jax import lax
from jax.experimental import pallas as pl
from jax.experimental.pallas import tpu as pltpu

F32 = jnp.float32
BF16 = jnp.bfloat16

RMS_EPS = 1e-6
STATE_BATCH = 8
SSM_GROUPS_PER_BLOCK = 8
FOX_HEAD_DIM = 128
TOP_K = 2
DISPATCH_BLOCK = 256
LANES = 128
VMEM_LIMIT = 56 * 1024 * 1024


def _params(*sem):
    return pltpu.CompilerParams(dimension_semantics=sem, vmem_limit_bytes=VMEM_LIMIT)


def _rms(x, gain):
    ms = jnp.mean(x * x, axis=-1, keepdims=True)
    return x * lax.rsqrt(ms + RMS_EPS) * gain


def _rms_matmul_kernel(x_ref, g_ref, w_ref, cs_ref, o_ref, xn_ref):
    @pl.when(pl.program_id(2) == 0)
    def _():
        xn_ref[...] = _rms(x_ref[...], g_ref[...]).astype(BF16)

    acc = jnp.dot(xn_ref[...], w_ref[...], preferred_element_type=F32)
    o_ref[...] = (acc * cs_ref[...]).astype(o_ref.dtype)


def _rms_matmul(x, gain, w, col_scale, out_dtype, time_major, ts, tn):
    b, s, d = x.shape
    m = w.shape[1]
    ts, tn = min(ts, s), min(tn, m)
    nj = m // tn
    if time_major:
        out_shape = jax.ShapeDtypeStruct((s, b * m), out_dtype)
        out_spec = pl.BlockSpec((ts, tn), lambda bi, si, j: (si, bi * nj + j))
    else:
        ns = s // ts
        out_shape = jax.ShapeDtypeStruct((b * s, m), out_dtype)
        out_spec = pl.BlockSpec((ts, tn), lambda bi, si, j: (bi * ns + si, j))
    return pl.pallas_call(
        _rms_matmul_kernel,
        grid=(b, s // ts, nj),
        in_specs=[
            pl.BlockSpec((None, ts, d), lambda bi, si, j: (bi, si, 0)),
            pl.BlockSpec((1, d), lambda bi, si, j: (0, 0)),
            pl.BlockSpec((d, tn), lambda bi, si, j: (0, j)),
            pl.BlockSpec((1, tn), lambda bi, si, j: (0, j)),
        ],
        out_specs=out_spec,
        out_shape=out_shape,
        scratch_shapes=[pltpu.VMEM((ts, d), BF16)],
        compiler_params=_params("parallel", "parallel", "arbitrary"),
        name="rms_matmul",
    )(x, gain.reshape(1, d), w, col_scale.reshape(1, m))


def _s5_scan_kernel(u_ref, bmat_ref, cmat_ref, are_ref, aim_ref, d_ref, y_ref, bu_ref, st_ref, *, n_blocks, t_chunk):
    @pl.when(pl.program_id(0) == 0)
    def _():
        st_ref[...] = jnp.zeros_like(st_ref)

    cw = bmat_ref.shape[1]
    sw = bmat_ref.shape[2] // 2
    for gb in range(n_blocks):
        u_blk = u_ref[:, gb * cw:(gb + 1) * cw]
        bu_ref[...] = jnp.dot(u_blk.astype(BF16), bmat_ref[gb], preferred_element_type=F32)
        a_re = jnp.broadcast_to(are_ref[gb], (STATE_BATCH, sw))
        a_im = jnp.broadcast_to(aim_ref[gb], (STATE_BATCH, sw))

        def step(t, carry):
            s_re, s_im = carry
            r0 = pl.multiple_of(t * STATE_BATCH, STATE_BATCH)
            b_re = bu_ref[pl.ds(r0, STATE_BATCH), 0:sw]
            b_im = bu_ref[pl.ds(r0, STATE_BATCH), sw:2 * sw]
            n_re = a_re * s_re - a_im * s_im + b_re
            n_im = a_re * s_im + a_im * s_re + b_im
            bu_ref[pl.ds(r0, STATE_BATCH), 0:sw] = n_re
            bu_ref[pl.ds(r0, STATE_BATCH), sw:2 * sw] = n_im
            return n_re, n_im

        s_re, s_im = lax.fori_loop(0, t_chunk, step, (st_ref[gb, 0], st_ref[gb, 1]), unroll=8)
        st_ref[gb, 0] = s_re
        st_ref[gb, 1] = s_im
        y = jnp.dot(bu_ref[...].astype(BF16), cmat_ref[gb], preferred_element_type=F32)
        y_ref[:, gb * cw:(gb + 1) * cw] = y + d_ref[:, gb * cw:(gb + 1) * cw] * u_blk


def _s5_scan(u_tm, bmat, cmat, a_re, a_im, d_skip, t_chunk):
    rows, ds = u_tm.shape
    seq = rows // STATE_BATCH
    t_chunk = min(t_chunk, seq)
    nb, cw, sw2 = bmat.shape
    tr = t_chunk * STATE_BATCH
    kern = functools.partial(_s5_scan_kernel, n_blocks=nb, t_chunk=t_chunk)
    return pl.pallas_call(
        kern,
        grid=(seq // t_chunk,),
        in_specs=[
            pl.BlockSpec((tr, ds), lambda i: (i, 0)),
            pl.BlockSpec((nb, cw, sw2), lambda i: (0, 0, 0)),
            pl.BlockSpec((nb, sw2, cw), lambda i: (0, 0, 0)),
            pl.BlockSpec((nb, 1, sw2 // 2), lambda i: (0, 0, 0)),
            pl.BlockSpec((nb, 1, sw2 // 2), lambda i: (0, 0, 0)),
            pl.BlockSpec((1, ds), lambda i: (0, 0)),
        ],
        out_specs=pl.BlockSpec((tr, ds), lambda i: (i, 0)),
        out_shape=jax.ShapeDtypeStruct((rows, ds), F32),
        scratch_shapes=[
            pltpu.VMEM((tr, sw2), F32),
            pltpu.VMEM((nb, 2, STATE_BATCH, sw2 // 2), F32),
        ],
        compiler_params=_params("arbitrary"),
        name="s5_scan",
    )(u_tm, bmat, cmat, a_re, a_im, d_skip.reshape(1, ds))


def _s5_discretise(lam_re, lam_im, b_re, b_im, c_re, c_im, log_step):
    lr, li = lam_re.astype(F32), lam_im.astype(F32)
    step = jnp.exp(log_step.astype(F32))[:, None]
    mag = jnp.exp(lr * step)
    ab_re = mag * jnp.cos(li * step)
    ab_im = mag * jnp.sin(li * step)
    den = lr * lr + li * li
    nr, ni = ab_re - 1.0, ab_im
    fr = (nr * lr + ni * li) / den
    fi = (ni * lr - nr * li) / den
    br, bi = b_re.astype(F32), b_im.astype(F32)
    bb_re = fr[..., None] * br - fi[..., None] * bi
    bb_im = fr[..., None] * bi + fi[..., None] * br
    g, n, c = bb_re.shape
    gpb = min(SSM_GROUPS_PER_BLOCK, g)
    nb = g // gpb
    eye = jnp.eye(gpb, dtype=F32)

    def in_blocks(bb):
        return jnp.einsum("bjnc,jk->bjckn", bb.reshape(nb, gpb, n, c), eye).reshape(nb, gpb * c, gpb * n)

    def out_blocks(cc):
        return jnp.einsum("bjcn,jk->bjnkc", cc.reshape(nb, gpb, c, n), eye).reshape(nb, gpb * n, gpb * c)

    bmat = jnp.concatenate([in_blocks(bb_re), in_blocks(bb_im)], axis=-1).astype(BF16)
    cmat = jnp.concatenate([out_blocks(c_re.astype(F32)), -out_blocks(c_im.astype(F32))], axis=1).astype(BF16)
    return bmat, cmat, ab_re.reshape(nb, 1, gpb * n), ab_im.reshape(nb, 1, gpb * n)


def _s5_out_kernel(y_ref, x_ref, wg_ref, bg_ref, wo_ref, o_ref):
    g = jax.nn.gelu(y_ref[...])
    z = jnp.dot(g.astype(BF16), wg_ref[...], preferred_element_type=F32) + bg_ref[...]
    gated = g * jax.nn.sigmoid(z)
    o_ref[...] = x_ref[...] + jnp.dot(gated.astype(BF16), wo_ref[...], preferred_element_type=F32)


def _s5_out(y_tm, x, w_glu, b_glu, w_out, ts):
    b, s, d = x.shape
    ds = w_glu.shape[0]
    ts = min(ts, s)
    return pl.pallas_call(
        _s5_out_kernel,
        grid=(b, s // ts),
        in_specs=[
            pl.BlockSpec((ts, ds), lambda bi, si: (si, bi)),
            pl.BlockSpec((None, ts, d), lambda bi, si: (bi, si, 0)),
            pl.BlockSpec((ds, ds), lambda bi, si: (0, 0)),
            pl.BlockSpec((1, ds), lambda bi, si: (0, 0)),
            pl.BlockSpec((ds, d), lambda bi, si: (0, 0)),
        ],
        out_specs=pl.BlockSpec((None, ts, d), lambda bi, si: (bi, si, 0)),
        out_shape=jax.ShapeDtypeStruct((b, s, d), F32),
        compiler_params=_params("parallel", "parallel"),
        name="s5_out",
    )(y_tm, x, w_glu, b_glu.reshape(1, ds), w_out)


def _router_kernel(h_ref, g_ref, w_ref, hn_ref, lg_ref):
    hn = _rms(h_ref[...], g_ref[...])
    hn_ref[...] = hn.astype(BF16)
    lg_ref[...] = jnp.dot(hn, w_ref[...], preferred_element_type=F32, precision=lax.Precision.HIGHEST)


def _router(h2, gain, w_router, tm):
    n, d = h2.shape
    tm = min(tm, n)
    wc = w_router.shape[1]
    return pl.pallas_call(
        _router_kernel,
        grid=(n // tm,),
        in_specs=[
            pl.BlockSpec((tm, d), lambda i: (i, 0)),
            pl.BlockSpec((1, d), lambda i: (0, 0)),
            pl.BlockSpec((d, wc), lambda i: (0, 0)),
        ],
        out_specs=[
            pl.BlockSpec((tm, d), lambda i: (i, 0)),
            pl.BlockSpec((tm, wc), lambda i: (i, 0)),
        ],
        out_shape=[jax.ShapeDtypeStruct((n, d), BF16), jax.ShapeDtypeStruct((n, wc), F32)],
        compiler_params=_params("parallel"),
        name="moe_router",
    )(h2, gain.reshape(1, d), w_router)


def _expert_kernel(be_ref, nv_ref, x_ref, wg_ref, wu_ref, wd_ref, o_ref, wgb_ref, wub_ref, wdb_ref):
    i = pl.program_id(0)
    valid = i < nv_ref[0]
    new_expert = jnp.logical_or(i == 0, be_ref[i] != be_ref[jnp.maximum(i - 1, 0)])

    @pl.when(jnp.logical_and(valid, new_expert))
    def _():
        wgb_ref[...] = wg_ref[...].astype(BF16)
        wub_ref[...] = wu_ref[...].astype(BF16)
        wdb_ref[...] = wd_ref[...].astype(BF16)

    @pl.when(valid)
    def _():
        x = x_ref[...]
        gate = jnp.dot(x, wgb_ref[...], preferred_element_type=F32)
        up = jnp.dot(x, wub_ref[...], preferred_element_type=F32)
        hid = jax.nn.silu(gate) * up
        o_ref[...] = jnp.dot(hid.astype(BF16), wdb_ref[...], preferred_element_type=F32).astype(o_ref.dtype)

    @pl.when(jnp.logical_not(valid))
    def _():
        o_ref[...] = jnp.zeros_like(o_ref)


def _experts(x_disp, block_expert, n_valid, w_gate, w_up, w_down):
    rows, d = x_disp.shape
    de = w_gate.shape[2]
    tb = DISPATCH_BLOCK
    grid_spec = pltpu.PrefetchScalarGridSpec(
        num_scalar_prefetch=2,
        grid=(rows // tb,),
        in_specs=[
            pl.BlockSpec((tb, d), lambda i, be, nv: (jnp.minimum(i, nv[0] - 1), 0)),
            pl.BlockSpec((None, d, de), lambda i, be, nv: (be[i], 0, 0)),
            pl.BlockSpec((None, d, de), lambda i, be, nv: (be[i], 0, 0)),
            pl.BlockSpec((None, de, d), lambda i, be, nv: (be[i], 0, 0)),
        ],
        out_specs=pl.BlockSpec((tb, d), lambda i, be, nv: (i, 0)),
        scratch_shapes=[
            pltpu.VMEM((d, de), BF16),
            pltpu.VMEM((d, de), BF16),
            pltpu.VMEM((de, d), BF16),
        ],
    )
    return pl.pallas_call(
        _expert_kernel,
        grid_spec=grid_spec,
        out_shape=jax.ShapeDtypeStruct((rows, d), BF16),
        compiler_params=_params("arbitrary"),
        name="moe_experts",
    )(block_expert, n_valid, x_disp, w_gate, w_up, w_down)


def _combine_kernel(h_ref, y_ref, w_ref, g_ref, o_ref, *, final_norm):
    d = h_ref.shape[1]
    w = w_ref[...]
    out = h_ref[...]
    for k in range(TOP_K):
        out = out + y_ref[:, k * d:(k + 1) * d].astype(F32) * w[:, k:k + 1]
    if final_norm:
        out = _rms(out, g_ref[...])
    o_ref[...] = out


def _combine(h2, y_pairs, top_w, gain, final_norm, tm):
    n, d = h2.shape
    tm = min(tm, n)
    kern = functools.partial(_combine_kernel, final_norm=final_norm)
    return pl.pallas_call(
        kern,
        grid=(n // tm,),
        in_specs=[
            pl.BlockSpec((tm, d), lambda i: (i, 0)),
            pl.BlockSpec((tm, TOP_K * d), lambda i: (i, 0)),
            pl.BlockSpec((tm, TOP_K), lambda i: (i, 0)),
            pl.BlockSpec((1, d), lambda i: (0, 0)),
        ],
        out_specs=pl.BlockSpec((tm, d), lambda i: (i, 0)),
        out_shape=jax.ShapeDtypeStruct((n, d), F32),
        compiler_params=_params("parallel"),
        name="moe_combine",
    )(h2, y_pairs, top_w, gain.reshape(1, d))


def _hier_moe(h2, norm_gain, w_group, b_group, w_expert, b_expert, w_gate, w_up, w_down, final_gain):
    n, d = h2.shape
    n_groups = w_group.shape[1]
    n_experts = w_expert.shape[1]
    epg = n_experts // n_groups
    n_logits = n_groups + n_experts
    wc = -(-n_logits // LANES) * LANES
    w_router = jnp.concatenate(
        [w_group.astype(F32), w_expert.astype(F32), jnp.zeros((d, wc - n_logits), F32)], axis=1)
    hn, logits = _router(h2, norm_gain, w_router, tm=512)

    group_logits = logits[:, :n_groups] + b_group.astype(F32)
    group_probs = jax.nn.softmax(group_logits, axis=-1)
    g_sel = jnp.argmax(group_logits, axis=-1)
    p_group = jnp.take_along_axis(group_probs, g_sel[:, None], axis=1)[:, 0]
    expert_logits = (logits[:, n_groups:n_logits] + b_expert.astype(F32)).reshape(n, n_groups, epg)
    in_group = jnp.take_along_axis(expert_logits, g_sel[:, None, None], axis=1)[:, 0]
    top_logits, top_local = lax.top_k(in_group, TOP_K)
    top_w = jax.nn.softmax(top_logits, axis=-1) * p_group[:, None]
    expert_ids = (g_sel[:, None] * epg + top_local).astype(jnp.int32)

    tb = DISPATCH_BLOCK
    n_pairs = n * TOP_K
    flat_e = expert_ids.reshape(-1)
    flat_tok = jnp.repeat(jnp.arange(n, dtype=jnp.int32), TOP_K)
    onehot = (flat_e[:, None] == jnp.arange(n_experts, dtype=jnp.int32)[None, :]).astype(jnp.int32)
    running = jnp.cumsum(onehot, axis=0)
    pos = jnp.take_along_axis(running, flat_e[:, None], axis=1)[:, 0] - 1
    counts = running[-1]
    blocks_per_expert = (counts + tb - 1) // tb
    block_end = jnp.cumsum(blocks_per_expert)
    block_start = block_end - blocks_per_expert
    dest = block_start[flat_e] * tb + pos
    n_blocks = -(-n_pairs // tb) + n_experts
    n_valid = block_end[-1:].astype(jnp.int32)
    blk = jnp.minimum(jnp.arange(n_blocks, dtype=jnp.int32), n_valid[0] - 1)
    block_expert = jnp.searchsorted(block_end, blk, side="right").astype(jnp.int32)
    tok_disp = jnp.zeros((n_blocks * tb,), jnp.int32).at[dest].set(flat_tok)

    x_disp = jnp.take(hn, tok_disp, axis=0)
    y_disp = _experts(x_disp, block_expert, n_valid, w_gate, w_up, w_down)
    y_pairs = jnp.take(y_disp, dest, axis=0).reshape(n, TOP_K * d)
    gain = norm_gain if final_gain is None else final_gain
    return _combine(h2, y_pairs, top_w, gain, final_gain is not None, tm=512)


def _forget_kernel(f_ref, b_ref, c_ref):
    x = jax.nn.log_sigmoid(f_ref[...] + b_ref[...])
    s = x.shape[1]
    lane = lax.broadcasted_iota(jnp.int32, x.shape, 1)
    shift = 1
    while shift < s:
        x = x + jnp.where(lane >= shift, pltpu.roll(x, shift, axis=1), 0.0)
        shift *= 2
    c_ref[...] = x


def _forget_cumsum(f_logit_t, b_forget):
    b, h, s = f_logit_t.shape
    return pl.pallas_call(
        _forget_kernel,
        grid=(b,),
        in_specs=[
            pl.BlockSpec((None, h, s), lambda i: (i, 0, 0)),
            pl.BlockSpec((h, 1), lambda i: (0, 0)),
        ],
        out_specs=pl.BlockSpec((None, h, s), lambda i: (i, 0, 0)),
        out_shape=jax.ShapeDtypeStruct((b, h, s), F32),
        compiler_params=_params("parallel"),
        name="fox_forget",
    )(f_logit_t, b_forget.astype(F32).reshape(h, 1))


def _fox_attn_kernel(q_ref, k_ref, v_ref, c_ref, o_ref, *, tq):
    seq = q_ref.shape[0]
    head = pl.program_id(1)
    neg_ck = -c_ref[pl.ds(head, 1), :]
    row = lax.broadcasted_iota(jnp.int32, (tq, tq), 0)
    col = lax.broadcasted_iota(jnp.int32, (tq, tq), 1)
    causal = col <= row

    def attend(q, k, v, bias, carry, mask):
        m, l, acc = carry
        s = lax.dot_general(q, k, (((1,), (1,)), ((), ())), preferred_element_type=F32) + bias
        if mask:
            s = jnp.where(causal, s, -jnp.inf)
        m_new = jnp.maximum(m, jnp.max(s, axis=-1, keepdims=True))
        alpha = jnp.exp(m - m_new)
        p = jnp.exp(s - m_new)
        l = alpha * l + jnp.sum(p, axis=-1, keepdims=True)
        acc = alpha * acc + jnp.dot(p.astype(BF16), v, preferred_element_type=F32)
        return m_new, l, acc

    for qi in range(seq // tq):
        q = q_ref[qi * tq:(qi + 1) * tq, :]
        carry = (jnp.full((tq, 1), -jnp.inf, F32), jnp.zeros((tq, 1), F32), jnp.zeros((tq, q.shape[1]), F32))
        sl = slice(qi * tq, (qi + 1) * tq)
        carry = attend(q, k_ref[sl, :], v_ref[sl, :], neg_ck[:, sl], carry, True)

        def kv_step(ki, carry, q=q):
            k0 = pl.multiple_of(ki * tq, tq)
            bias = -c_ref[pl.ds(head, 1), pl.ds(k0, tq)]
            return attend(q, k_ref[pl.ds(k0, tq), :], v_ref[pl.ds(k0, tq), :], bias, carry, False)

        if qi > 0:
            carry = lax.fori_loop(0, qi, kv_step, carry)
        _, l, acc = carry
        o_ref[sl, :] = (acc / l).astype(o_ref.dtype)


def _fox_attention(proj, cum, batch, seq, heads, tq):
    dh = FOX_HEAD_DIM
    tq = min(tq, seq)
    kern = functools.partial(_fox_attn_kernel, tq=tq)
    return pl.pallas_call(
        kern,
        grid=(batch, heads),
        in_specs=[
            pl.BlockSpec((seq, dh), lambda b, h: (b, h)),
            pl.BlockSpec((seq, dh), lambda b, h: (b, heads + h)),
            pl.BlockSpec((seq, dh), lambda b, h: (b, 2 * heads + h)),
            pl.BlockSpec((None, heads, seq), lambda b, h: (b, 0, 0)),
        ],
        out_specs=pl.BlockSpec((seq, dh), lambda b, h: (b, h)),
        out_shape=jax.ShapeDtypeStruct((batch * seq, heads * dh), BF16),
        compiler_params=_params("parallel", "parallel"),
        name="fox_attention",
    )(proj, proj, proj, cum)


def _fox_out_kernel(o_ref, gate_ref, w_ref, h_ref, out_ref, og_ref):
    @pl.when(pl.program_id(1) == 0)
    def _():
        og_ref[...] = (o_ref[...].astype(F32) * jax.nn.sigmoid(gate_ref[...].astype(F32))).astype(BF16)

    out_ref[...] = h_ref[...] + jnp.dot(og_ref[...], w_ref[...], preferred_element_type=F32)


def _fox_out(o, proj, w_out, h2, tm, tn):
    n, d = h2.shape
    tm, tn = min(tm, n), min(tn, d)
    gate_block = 3
    return pl.pallas_call(
        _fox_out_kernel,
        grid=(n // tm, d // tn),
        in_specs=[
            pl.BlockSpec((tm, d), lambda i, j: (i, 0)),
            pl.BlockSpec((tm, d), lambda i, j: (i, gate_block)),
            pl.BlockSpec((d, tn), lambda i, j: (0, j)),
            pl.BlockSpec((tm, tn), lambda i, j: (i, j)),
        ],
        out_specs=pl.BlockSpec((tm, tn), lambda i, j: (i, j)),
        out_shape=jax.ShapeDtypeStruct((n, d), F32),
        scratch_shapes=[pltpu.VMEM((tm, d), BF16)],
        compiler_params=_params("parallel", "arbitrary"),
        name="fox_out",
    )(o, proj, w_out, h2)


def _s5_layer(x, norm_gain, w_in, lam_re, lam_im, b_re, b_im, c_re, c_im, d_skip, log_step, w_glu, b_glu, w_out):
    b, s, d = x.shape
    assert b == STATE_BATCH
    ds = w_in.shape[1]
    bmat, cmat, a_re, a_im = _s5_discretise(lam_re, lam_im, b_re, b_im, c_re, c_im, log_step)
    u_tm = _rms_matmul(x, norm_gain, w_in.astype(BF16), jnp.ones((ds,), F32), F32, True, ts=1024, tn=ds)
    y_tm = _s5_scan(u_tm.reshape(s * b, ds), bmat, cmat, a_re, a_im, d_skip.astype(F32), t_chunk=128)
    return _s5_out(y_tm.reshape(s, b * ds), x, w_glu.astype(BF16), b_glu.astype(F32), w_out.astype(BF16), ts=512)


def _fox_layer(h, norm_gain, w_in, b_forget, w_out):
    b, s, d = h.shape
    heads = d // FOX_HEAD_DIM
    w_main = w_in[:, :4 * d].astype(BF16)
    w_forget = jnp.pad(w_in[:, 4 * d:], ((0, 0), (0, LANES - heads))).astype(BF16)
    col_scale = jnp.concatenate([jnp.full((d,), FOX_HEAD_DIM ** -0.5, F32), jnp.ones((3 * d,), F32)])
    proj = _rms_matmul(h, norm_gain, w_main, col_scale, BF16, False, ts=1024, tn=1024)
    f_logit = _rms_matmul(h, norm_gain, w_forget, jnp.ones((LANES,), F32), F32, False, ts=1024, tn=LANES)
    f_logit_t = f_logit[:, :heads].reshape(b, s, heads).transpose(0, 2, 1)
    cum = _forget_cumsum(f_logit_t, b_forget)
    o = _fox_attention(proj, cum, b, s, heads, tq=512)
    return _fox_out(o, proj, w_out.astype(BF16), h.reshape(b * s, d), tm=512, tn=1024).reshape(b, s, d)


def kernel(x, l0_mix_norm, l0_s5_w_in, l0_s5_lambda_re, l0_s5_lambda_im, l0_s5_b_re, l0_s5_b_im, l0_s5_c_re, l0_s5_c_im, l0_s5_d, l0_s5_log_step, l0_s5_w_glu, l0_s5_b_glu, l0_s5_w_out, l0_ffn_norm, l0_moe_w_group, l0_moe_b_group, l0_moe_w_expert, l0_moe_b_expert, l0_moe_w_gate, l0_moe_w_up, l0_moe_w_down, l1_mix_norm, l1_fox_w_in, l1_fox_b_forget, l1_fox_w_out, l1_ffn_norm, l1_moe_w_group, l1_moe_b_group, l1_moe_w_expert, l1_moe_b_expert, l1_moe_w_gate, l1_moe_w_up, l1_moe_w_down, final_norm):
    b, s, d = x.shape
    h = _s5_layer(x, l0_mix_norm, l0_s5_w_in, l0_s5_lambda_re, l0_s5_lambda_im, l0_s5_b_re, l0_s5_b_im,
                  l0_s5_c_re, l0_s5_c_im, l0_s5_d, l0_s5_log_step, l0_s5_w_glu, l0_s5_b_glu, l0_s5_w_out)
    h = _hier_moe(h.reshape(b * s, d), l0_ffn_norm, l0_moe_w_group, l0_moe_b_group, l0_moe_w_expert,
                  l0_moe_b_expert, l0_moe_w_gate, l0_moe_w_up, l0_moe_w_down, None).reshape(b, s, d)
    h = _fox_layer(h, l1_mix_norm, l1_fox_w_in, l1_fox_b_forget, l1_fox_w_out)
    h = _hier_moe(h.reshape(b * s, d), l1_ffn_norm, l1_moe_w_group, l1_moe_b_group, l1_moe_w_expert,
                  l1_moe_b_expert, l1_moe_w_gate, l1_moe_w_up, l1_moe_w_down, final_norm)
    return h.reshape(b, s, d)
```

```python
import functools
import math

import jax
import jax.numpy as jnp
from jax import lax
from jax.experimental import pallas as pl
from jax.experimental.pallas import tpu as pltpu

F32 = jnp.float32
BF16 = jnp.bfloat16

RMS_EPS = 1e-6
STATE_BATCH = 8
SSM_GROUPS_PER_BLOCK = 8
FOX_HEAD_DIM = 128
TOP_K = 2
DISPATCH_BLOCK = 256
LANES = 128
VMEM_LIMIT = 56 * 1024 * 1024


def _params(*sem):
    return pltpu.CompilerParams(dimension_semantics=sem, vmem_limit_bytes=VMEM_LIMIT)


def _rms(x, gain):
    ms = jnp.mean(x * x, axis=-1, keepdims=True)
    return x * lax.rsqrt(ms + RMS_EPS) * gain


def _rms_matmul_kernel(x_ref, g_ref, w_ref, cs_ref, o_ref, xn_ref):
    @pl.when(pl.program_id(2) == 0)
    def _():
        xn_ref[...] = _rms(x_ref[...], g_ref[...]).astype(BF16)

    acc = jnp.dot(xn_ref[...], w_ref[...], preferred_element_type=F32)
    o_ref[...] = (acc * cs_ref[...]).astype(o_ref.dtype)


def _rms_matmul(x, gain, w, col_scale, out_dtype, time_major, ts, tn):
    b, s, d = x.shape
    m = w.shape[1]
    ts, tn = min(ts, s), min(tn, m)
    nj = m // tn
    if time_major:
        out_shape = jax.ShapeDtypeStruct((s, b * m), out_dtype)
        out_spec = pl.BlockSpec((ts, tn), lambda bi, si, j: (si, bi * nj + j))
    else:
        ns = s // ts
        out_shape = jax.ShapeDtypeStruct((b * s, m), out_dtype)
        out_spec = pl.BlockSpec((ts, tn), lambda bi, si, j: (bi * ns + si, j))
    return pl.pallas_call(
        _rms_matmul_kernel,
        grid=(b, s // ts, nj),
        in_specs=[
            pl.BlockSpec((None, ts, d), lambda bi, si, j: (bi, si, 0)),
            pl.BlockSpec((1, d), lambda bi, si, j: (0, 0)),
            pl.BlockSpec((d, tn), lambda bi, si, j: (0, j)),
            pl.BlockSpec((1, tn), lambda bi, si, j: (0, j)),
        ],
        out_specs=out_spec,
        out_shape=out_shape,
        scratch_shapes=[pltpu.VMEM((ts, d), BF16)],
        compiler_params=_params("parallel", "parallel", "arbitrary"),
        name="rms_matmul",
    )(x, gain.reshape(1, d), w, col_scale.reshape(1, m))


def _s5_scan_kernel(u_ref, bmat_ref, cmat_ref, are_ref, aim_ref, d_ref, y_ref, bu_ref, st_ref, *, n_blocks, t_chunk):
    @pl.when(pl.program_id(0) == 0)
    def _():
        st_ref[...] = jnp.zeros_like(st_ref)

    cw = bmat_ref.shape[1]
    sw = bmat_ref.shape[2] // 2
    for gb in range(n_blocks):
        u_blk = u_ref[:, gb * cw:(gb + 1) * cw]
        bu_ref[...] = jnp.dot(u_blk.astype(BF16), bmat_ref[gb], preferred_element_type=F32)
        a_re = jnp.broadcast_to(are_ref[gb], (STATE_BATCH, sw))
        a_im = jnp.broadcast_to(aim_ref[gb], (STATE_BATCH, sw))

        def step(t, carry):
            s_re, s_im = carry
            r0 = pl.multiple_of(t * STATE_BATCH, STATE_BATCH)
            b_re = bu_ref[pl.ds(r0, STATE_BATCH), 0:sw]
            b_im = bu_ref[pl.ds(r0, STATE_BATCH), sw:2 * sw]
            n_re = a_re * s_re - a_im * s_im + b_re
            n_im = a_re * s_im + a_im * s_re + b_im
            bu_ref[pl.ds(r0, STATE_BATCH), 0:sw] = n_re
            bu_ref[pl.ds(r0, STATE_BATCH), sw:2 * sw] = n_im
            return n_re, n_im

        s_re, s_im = lax.fori_loop(0, t_chunk, step, (st_ref[gb, 0], st_ref[gb, 1]), unroll=8)
        st_ref[gb, 0] = s_re
        st_ref[gb, 1] = s_im
        y = jnp.dot(bu_ref[...].astype(BF16), cmat_ref[gb], preferred_element_type=F32)
        y_ref[:, gb * cw:(gb + 1) * cw] = y + d_ref[:, gb * cw:(gb + 1) * cw] * u_blk


def _s5_scan(u_tm, bmat, cmat, a_re, a_im, d_skip, t_chunk):
    rows, ds = u_tm.shape
    seq = rows // STATE_BATCH
    t_chunk = min(t_chunk, seq)
    nb, cw, sw2 = bmat.shape
    tr = t_chunk * STATE_BATCH
    kern = functools.partial(_s5_scan_kernel, n_blocks=nb, t_chunk=t_chunk)
    return pl.pallas_call(
        kern,
        grid=(seq // t_chunk,),
        in_specs=[
            pl.BlockSpec((tr, ds), lambda i: (i, 0)),
            pl.BlockSpec((nb, cw, sw2), lambda i: (0, 0, 0)),
            pl.BlockSpec((nb, sw2, cw), lambda i: (0, 0, 0)),
            pl.BlockSpec((nb, 1, sw2 // 2), lambda i: (0, 0, 0)),
            pl.BlockSpec((nb, 1, sw2 // 2), lambda i: (0, 0, 0)),
            pl.BlockSpec((1, ds), lambda i: (0, 0)),
        ],
        out_specs=pl.BlockSpec((tr, ds), lambda i: (i, 0)),
        out_shape=jax.ShapeDtypeStruct((rows, ds), F32),
        scratch_shapes=[
            pltpu.VMEM((tr, sw2), F32),
            pltpu.VMEM((nb, 2, STATE_BATCH, sw2 // 2), F32),
        ],
        compiler_params=_params("arbitrary"),
        name="s5_scan",
    )(u_tm, bmat, cmat, a_re, a_im, d_skip.reshape(1, ds))


def _s5_discretise(lam_re, lam_im, b_re, b_im, c_re, c_im, log_step):
    lr, li = lam_re.astype(F32), lam_im.astype(F32)
    step = jnp.exp(log_step.astype(F32))[:, None]
    mag = jnp.exp(lr * step)
    ab_re = mag * jnp.cos(li * step)
    ab_im = mag * jnp.sin(li * step)
    den = lr * lr + li * li
    nr, ni = ab_re - 1.0, ab_im
    fr = (nr * lr + ni * li) / den
    fi = (ni * lr - nr * li) / den
    br, bi = b_re.astype(F32), b_im.astype(F32)
    bb_re = fr[..., None] * br - fi[..., None] * bi
    bb_im = fr[..., None] * bi + fi[..., None] * br
    g, n, c = bb_re.shape
    gpb = min(SSM_GROUPS_PER_BLOCK, g)
    nb = g // gpb
    eye = jnp.eye(gpb, dtype=F32)

    def in_blocks(bb):
        return jnp.einsum("bjnc,jk->bjckn", bb.reshape(nb, gpb, n, c), eye).reshape(nb, gpb * c, gpb * n)

    def out_blocks(cc):
        return jnp.einsum("bjcn,jk->bjnkc", cc.reshape(nb, gpb, c, n), eye).reshape(nb, gpb * n, gpb * c)

    bmat = jnp.concatenate([in_blocks(bb_re), in_blocks(bb_im)], axis=-1).astype(BF16)
    cmat = jnp.concatenate([out_blocks(c_re.astype(F32)), -out_blocks(c_im.astype(F32))], axis=1).astype(BF16)
    return bmat, cmat, ab_re.reshape(nb, 1, gpb * n), ab_im.reshape(nb, 1, gpb * n)


def _s5_out_kernel(y_ref, x_ref, wg_ref, bg_ref, wo_ref, o_ref):
    g = jax.nn.gelu(y_ref[...])
    z = jnp.dot(g.astype(BF16), wg_ref[...], preferred_element_type=F32) + bg_ref[...]
    gated = g * jax.nn.sigmoid(z)
    o_ref[...] = x_ref[...] + jnp.dot(gated.astype(BF16), wo_ref[...], preferred_element_type=F32)


def _s5_out(y_tm, x, w_glu, b_glu, w_out, ts):
    b, s, d = x.shape
    ds = w_glu.shape[0]
    ts = min(ts, s)
    return pl.pallas_call(
        _s5_out_kernel,
        grid=(b, s // ts),
        in_specs=[
            pl.BlockSpec((ts, ds), lambda bi, si: (si, bi)),
            pl.BlockSpec((None, ts, d), lambda bi, si: (bi, si, 0)),
            pl.BlockSpec((ds, ds), lambda bi, si: (0, 0)),
            pl.BlockSpec((1, ds), lambda bi, si: (0, 0)),
            pl.BlockSpec((ds, d), lambda bi, si: (0, 0)),
        ],
        out_specs=pl.BlockSpec((None, ts, d), lambda bi, si: (bi, si, 0)),
        out_shape=jax.ShapeDtypeStruct((b, s, d), F32),
        compiler_params=_params("parallel", "parallel"),
        name="s5_out",
    )(y_tm, x, w_glu, b_glu.reshape(1, ds), w_out)


def _pack_halves(x):
    half = x.shape[-1] // 2
    lo = lax.bitcast_convert_type(x[:, :half].astype(BF16).astype(F32), jnp.uint32)
    hi = lax.bitcast_convert_type(x[:, half:].astype(BF16).astype(F32), jnp.uint32)
    return (lo >> 16) | (hi & jnp.uint32(0xFFFF0000))


def _unpack_halves(words):
    lo = lax.bitcast_convert_type(words << 16, F32)
    hi = lax.bitcast_convert_type(words & jnp.uint32(0xFFFF0000), F32)
    return lo, hi


def _router_kernel(h_ref, g_ref, w_ref, b_ref, tri_ref, hn_ref, ei_ref, ew_ref, cnt_ref, base_ref,
                   *, n_groups, epg):
    @pl.when(pl.program_id(0) == 0)
    def _():
        base_ref[...] = jnp.zeros_like(base_ref)

    hn = _rms(h_ref[...], g_ref[...])
    hn_ref[...] = _pack_halves(hn)
    logits = jnp.dot(hn, w_ref[...], preferred_element_type=F32, precision=lax.Precision.HIGHEST) + b_ref[...]
    lane = lax.broadcasted_iota(jnp.int32, logits.shape, 1)
    n_lanes = logits.shape[1]

    def first_argmax(vals, vmax):
        first = jnp.min(jnp.where(vals == vmax, lane, n_lanes).astype(F32), axis=-1, keepdims=True)
        return first.astype(jnp.int32)

    in_groups = lane < n_groups
    gl = jnp.where(in_groups, logits, -jnp.inf)
    g_max = jnp.max(gl, axis=-1, keepdims=True)
    g_sel = first_argmax(gl, g_max)
    p_group = 1.0 / jnp.sum(jnp.where(in_groups, jnp.exp(logits - g_max), 0.0), axis=-1, keepdims=True)

    lo = n_groups + g_sel * epg
    el = jnp.where(jnp.logical_and(lane >= lo, lane < lo + epg), logits, -jnp.inf)
    t1 = jnp.max(el, axis=-1, keepdims=True)
    i1 = first_argmax(el, t1)
    el2 = jnp.where(lane == i1, -jnp.inf, el)
    t2 = jnp.max(el2, axis=-1, keepdims=True)
    i2 = first_argmax(el2, t2)
    e2x = jnp.exp(t2 - t1)
    w1 = p_group / (1.0 + e2x)
    w2 = p_group * e2x / (1.0 + e2x)
    e1 = i1 - n_groups
    e2 = i2 - n_groups

    onehot = jnp.logical_or(lane == e1, lane == e2)
    earlier = jnp.dot(tri_ref[...], jnp.where(onehot, 1.0, 0.0).astype(BF16), preferred_element_type=F32)
    total = earlier + base_ref[...]
    r1 = jnp.sum(jnp.where(lane == e1, total, 0.0), axis=-1, keepdims=True).astype(jnp.int32)
    r2 = jnp.sum(jnp.where(lane == e2, total, 0.0), axis=-1, keepdims=True).astype(jnp.int32)
    base_ref[...] = base_ref[...] + jnp.sum(jnp.where(onehot, 1.0, 0.0), axis=0, keepdims=True)
    cnt_ref[...] = base_ref[...].astype(jnp.int32)

    ei_ref[...] = jnp.where(lane == 0, e1, jnp.where(lane == 1, e2, jnp.where(lane == 2, r1, jnp.where(lane == 3, r2, 0))))
    ew_ref[...] = jnp.where(lane == 0, w1, jnp.where(lane == 1, w2, 0.0))


def _router(h2, gain, w_router, b_router, n_groups, epg, tm):
    n, d = h2.shape
    tm = min(tm, n)
    wc = w_router.shape[1]
    tri = jnp.tri(tm, k=-1, dtype=BF16)
    kern = functools.partial(_router_kernel, n_groups=n_groups, epg=epg)
    return pl.pallas_call(
        kern,
        grid=(n // tm,),
        in_specs=[
            pl.BlockSpec((tm, d), lambda i: (i, 0)),
            pl.BlockSpec((1, d), lambda i: (0, 0)),
            pl.BlockSpec((d, wc), lambda i: (0, 0)),
            pl.BlockSpec((1, wc), lambda i: (0, 0)),
            pl.BlockSpec((tm, tm), lambda i: (0, 0)),
        ],
        out_specs=[
            pl.BlockSpec((tm, d // 2), lambda i: (i, 0)),
            pl.BlockSpec((tm, wc), lambda i: (i, 0)),
            pl.BlockSpec((tm, wc), lambda i: (i, 0)),
            pl.BlockSpec((1, wc), lambda i: (0, 0)),
        ],
        out_shape=[
            jax.ShapeDtypeStruct((n, d // 2), jnp.uint32),
            jax.ShapeDtypeStruct((n, wc), jnp.int32),
            jax.ShapeDtypeStruct((n, wc), F32),
            jax.ShapeDtypeStruct((1, wc), jnp.int32),
        ],
        scratch_shapes=[pltpu.VMEM((1, wc), F32)],
        compiler_params=_params("arbitrary"),
        name="moe_router",
    )(h2, gain.reshape(1, d), w_router, b_router, tri)


def _dispatch_kernel(dest_ref, last_ref, hn_ref, x_ref, zero_ref, zsem, rsem, *, n_pairs, n_experts, tb, chunk):
    zero_ref[...] = jnp.zeros_like(zero_ref)

    def zero_copy(e):
        row0 = pl.multiple_of(jnp.maximum(last_ref[e], 0) * tb, tb)
        return pltpu.make_async_copy(zero_ref, x_ref.at[pl.ds(row0, tb), :], zsem.at[e])

    def zero_start(e, c):
        @pl.when(last_ref[e] >= 0)
        def _():
            zero_copy(e).start()
        return c

    def zero_wait(e, c):
        @pl.when(last_ref[e] >= 0)
        def _():
            zero_copy(e).wait()
        return c

    lax.fori_loop(0, n_experts, zero_start, 0)
    lax.fori_loop(0, n_experts, zero_wait, 0)

    def row_start(p, slot):
        pltpu.make_async_copy(hn_ref.at[pl.ds(p // TOP_K, 1), :], x_ref.at[pl.ds(dest_ref[p], 1), :],
                              rsem.at[slot]).start()

    def row_wait(slot):
        pltpu.make_async_copy(hn_ref.at[pl.ds(0, 1), :], x_ref.at[pl.ds(0, 1), :], rsem.at[slot]).wait()

    n_chunks = n_pairs // chunk

    def chunk_body(c, carry):
        slot = c % 2

        def start_one(j, cc):
            row_start(c * chunk + j, slot)
            return cc

        lax.fori_loop(0, chunk, start_one, 0, unroll=8)

        @pl.when(c > 0)
        def _():
            lax.fori_loop(0, chunk, lambda j, cc: (row_wait(1 - slot), cc)[1], 0, unroll=8)

        return carry

    lax.fori_loop(0, n_chunks, chunk_body, 0)
    lax.fori_loop(0, chunk, lambda j, cc: (row_wait((n_chunks - 1) % 2), cc)[1], 0, unroll=8)


def _dispatch(hn_packed, dest, last_block, n_rows):
    n, dw = hn_packed.shape
    n_pairs = dest.shape[0]
    n_experts = last_block.shape[0]
    chunk = min(256, n_pairs)
    kern = functools.partial(_dispatch_kernel, n_pairs=n_pairs, n_experts=n_experts, tb=DISPATCH_BLOCK, chunk=chunk)
    grid_spec = pltpu.PrefetchScalarGridSpec(
        num_scalar_prefetch=2,
        grid=(1,),
        in_specs=[pl.BlockSpec(memory_space=pl.ANY)],
        out_specs=pl.BlockSpec(memory_space=pl.ANY),
        scratch_shapes=[
            pltpu.VMEM((DISPATCH_BLOCK, dw), jnp.uint32),
            pltpu.SemaphoreType.DMA((n_experts,)),
            pltpu.SemaphoreType.DMA((2,)),
        ],
    )
    return pl.pallas_call(
        kern,
        grid_spec=grid_spec,
        out_shape=jax.ShapeDtypeStruct((n_rows, dw), jnp.uint32),
        compiler_params=_params("arbitrary"),
        name="moe_dispatch",
    )(dest, last_block, hn_packed)


def _expert_kernel(be_ref, nv_ref, x_ref, wg_ref, wu_ref, wd_ref, o_ref, wgb_ref, wub_ref, wdb_ref):
    i = pl.program_id(0)
    valid = i < nv_ref[0]
    new_expert = jnp.logical_or(i == 0, be_ref[i] != be_ref[jnp.maximum(i - 1, 0)])

    @pl.when(jnp.logical_and(valid, new_expert))
    def _():
        wgb_ref[...] = wg_ref[...].astype(BF16)
        wub_ref[...] = wu_ref[...].astype(BF16)
        wdb_ref[...] = wd_ref[...].astype(BF16)

    @pl.when(valid)
    def _():
        half = x_ref.shape[1]
        x_lo, x_hi = _unpack_halves(x_ref[...])
        x_lo, x_hi = x_lo.astype(BF16), x_hi.astype(BF16)

        def proj(w_ref):
            return (jnp.dot(x_lo, w_ref[:half, :], preferred_element_type=F32)
                    + jnp.dot(x_hi, w_ref[half:, :], preferred_element_type=F32))

        hid = jax.nn.silu(proj(wgb_ref)) * proj(wub_ref)
        o_ref[...] = _pack_halves(jnp.dot(hid.astype(BF16), wdb_ref[...], preferred_element_type=F32))

    @pl.when(jnp.logical_not(valid))
    def _():
        o_ref[...] = jnp.zeros_like(o_ref)


def _experts(x_disp, block_expert, n_valid, w_gate, w_up, w_down):
    rows, dw = x_disp.shape
    d, de = w_gate.shape[1], w_gate.shape[2]
    tb = DISPATCH_BLOCK
    grid_spec = pltpu.PrefetchScalarGridSpec(
        num_scalar_prefetch=2,
        grid=(rows // tb,),
        in_specs=[
            pl.BlockSpec((tb, dw), lambda i, be, nv: (jnp.minimum(i, nv[0] - 1), 0)),
            pl.BlockSpec((None, d, de), lambda i, be, nv: (be[i], 0, 0)),
            pl.BlockSpec((None, d, de), lambda i, be, nv: (be[i], 0, 0)),
            pl.BlockSpec((None, de, d), lambda i, be, nv: (be[i], 0, 0)),
        ],
        out_specs=pl.BlockSpec((tb, dw), lambda i, be, nv: (i, 0)),
        scratch_shapes=[
            pltpu.VMEM((d, de), BF16),
            pltpu.VMEM((d, de), BF16),
            pltpu.VMEM((de, d), BF16),
        ],
    )
    return pl.pallas_call(
        _expert_kernel,
        grid_spec=grid_spec,
        out_shape=jax.ShapeDtypeStruct((rows, dw), jnp.uint32),
        compiler_params=_params("arbitrary"),
        name="moe_experts",
    )(block_expert, n_valid, x_disp, w_gate, w_up, w_down)


def _combine_kernel(dest_ref, h_ref, ew_ref, g_ref, y_ref, o_ref, buf_ref, sem, *, final_norm):
    tm, d = h_ref.shape
    i = pl.program_id(0)
    n_tiles = pl.num_programs(0)

    def start_tile(tile, slot):
        def start_one(j, c):
            for k in range(TOP_K):
                row = dest_ref[(tile * tm + j) * TOP_K + k]
                pltpu.make_async_copy(y_ref.at[pl.ds(row, 1), :], buf_ref.at[slot, pl.ds(k * tm + j, 1), :],
                                      sem.at[slot]).start()
            return c

        lax.fori_loop(0, tm, start_one, 0, unroll=4)

    def wait_tile(slot):
        def wait_one(j, c):
            pltpu.make_async_copy(y_ref.at[pl.ds(0, 1), :], buf_ref.at[slot, pl.ds(0, 1), :], sem.at[slot]).wait()
            return c

        lax.fori_loop(0, TOP_K * tm, wait_one, 0, unroll=8)

    slot = i % 2

    @pl.when(i == 0)
    def _():
        start_tile(0, 0)

    @pl.when(i + 1 < n_tiles)
    def _():
        start_tile(i + 1, 1 - slot)

    wait_tile(slot)
    h = h_ref[...]
    w = ew_ref[...]
    lo, hi = h[:, :d // 2], h[:, d // 2:]
    for k in range(TOP_K):
        y_lo, y_hi = _unpack_halves(buf_ref[slot, k * tm:(k + 1) * tm, :])
        lo = lo + y_lo * w[:, k:k + 1]
        hi = hi + y_hi * w[:, k:k + 1]
    if final_norm:
        ms = (jnp.sum(lo * lo, axis=-1, keepdims=True) + jnp.sum(hi * hi, axis=-1, keepdims=True)) / d
        scale = lax.rsqrt(ms + RMS_EPS)
        lo = lo * scale * g_ref[:, :d // 2]
        hi = hi * scale * g_ref[:, d // 2:]
    o_ref[:, :d // 2] = lo
    o_ref[:, d // 2:] = hi


def _combine(h2, y_disp, dest, ew, gain, final_norm, tm):
    n, d = h2.shape
    tm = min(tm, n)
    dw = y_disp.shape[1]
    wc = ew.shape[1]
    kern = functools.partial(_combine_kernel, final_norm=final_norm)
    grid_spec = pltpu.PrefetchScalarGridSpec(
        num_scalar_prefetch=1,
        grid=(n // tm,),
        in_specs=[
            pl.BlockSpec((tm, d), lambda i, dest: (i, 0)),
            pl.BlockSpec((tm, wc), lambda i, dest: (i, 0)),
            pl.BlockSpec((1, d), lambda i, dest: (0, 0)),
            pl.BlockSpec(memory_space=pl.ANY),
        ],
        out_specs=pl.BlockSpec((tm, d), lambda i, dest: (i, 0)),
        scratch_shapes=[
            pltpu.VMEM((2, TOP_K * tm, dw), jnp.uint32),
            pltpu.SemaphoreType.DMA((2,)),
        ],
    )
    return pl.pallas_call(
        kern,
        grid_spec=grid_spec,
        out_shape=jax.ShapeDtypeStruct((n, d), F32),
        compiler_params=_params("arbitrary"),
        name="moe_combine",
    )(dest, h2, ew, gain.reshape(1, d), y_disp)


def _hier_moe(h2, norm_gain, w_group, b_group, w_expert, b_expert, w_gate, w_up, w_down, final_gain):
    n, d = h2.shape
    n_groups = w_group.shape[1]
    n_experts = w_expert.shape[1]
    epg = n_experts // n_groups
    n_logits = n_groups + n_experts
    wc = -(-n_logits // LANES) * LANES
    w_router = jnp.concatenate(
        [w_group.astype(F32), w_expert.astype(F32), jnp.zeros((d, wc - n_logits), F32)], axis=1)
    b_router = jnp.concatenate(
        [b_group.astype(F32), b_expert.astype(F32), jnp.zeros((wc - n_logits,), F32)]).reshape(1, wc)
    hn_packed, ei, ew, counts = _router(h2, norm_gain, w_router, b_router, n_groups, epg, tm=512)

    tb = DISPATCH_BLOCK
    n_pairs = n * TOP_K
    counts = counts[0, :n_experts]
    blocks_per_expert = (counts + tb - 1) // tb
    block_end = jnp.cumsum(blocks_per_expert)
    block_start = block_end - blocks_per_expert
    expert_id, rank = ei[:, :TOP_K], ei[:, TOP_K:2 * TOP_K]
    dest = (jnp.take(block_start, expert_id) * tb + rank).reshape(-1).astype(jnp.int32)
    last_block = jnp.where(blocks_per_expert > 0, block_end - 1, -1).astype(jnp.int32)
    n_blocks = -(-n_pairs // tb) + n_experts
    n_valid = block_end[-1:].astype(jnp.int32)
    blk = jnp.minimum(jnp.arange(n_blocks, dtype=jnp.int32), n_valid[0] - 1)
    block_expert = jnp.searchsorted(block_end, blk, side="right").astype(jnp.int32)

    x_disp = _dispatch(hn_packed, dest, last_block, n_blocks * tb)
    y_disp = _experts(x_disp, block_expert, n_valid, w_gate, w_up, w_down)
    gain = norm_gain if final_gain is None else final_gain
    return _combine(h2, y_disp, dest, ew, gain, final_gain is not None, tm=512)


def _forget_kernel(f_ref, b_ref, c_ref):
    x = jax.nn.log_sigmoid(f_ref[...] + b_ref[...])
    s = x.shape[1]
    lane = lax.broadcasted_iota(jnp.int32, x.shape, 1)
    shift = 1
    while shift < s:
        x = x + jnp.where(lane >= shift, pltpu.roll(x, shift, axis=1), 0.0)
        shift *= 2
    c_ref[...] = x


def _forget_cumsum(f_logit_t, b_forget):
    b, h, s = f_logit_t.shape
    return pl.pallas_call(
        _forget_kernel,
        grid=(b,),
        in_specs=[
            pl.BlockSpec((None, h, s), lambda i: (i, 0, 0)),
            pl.BlockSpec((h, 1), lambda i: (0, 0)),
        ],
        out_specs=pl.BlockSpec((None, h, s), lambda i: (i, 0, 0)),
        out_shape=jax.ShapeDtypeStruct((b, h, s), F32),
        compiler_params=_params("parallel"),
        name="fox_forget",
    )(f_logit_t, b_forget.astype(F32).reshape(h, 1))


def _fox_attn_kernel(q_ref, k_ref, v_ref, c_ref, o_ref, *, tq):
    seq = q_ref.shape[0]
    head = pl.program_id(1)
    neg_ck = -c_ref[pl.ds(head, 1), :]
    row = lax.broadcasted_iota(jnp.int32, (tq, tq), 0)
    col = lax.broadcasted_iota(jnp.int32, (tq, tq), 1)
    causal = col <= row

    def attend(q, k, v, bias, carry, mask):
        m, l, acc = carry
        s = lax.dot_general(q, k, (((1,), (1,)), ((), ())), preferred_element_type=F32) + bias
        if mask:
            s = jnp.where(causal, s, -jnp.inf)
        m_new = jnp.maximum(m, jnp.max(s, axis=-1, keepdims=True))
        alpha = jnp.exp(m - m_new)
        p = jnp.exp(s - m_new)
        l = alpha * l + jnp.sum(p, axis=-1, keepdims=True)
        acc = alpha * acc + jnp.dot(p.astype(BF16), v, preferred_element_type=F32)
        return m_new, l, acc

    for qi in range(seq // tq):
        q = q_ref[qi * tq:(qi + 1) * tq, :]
        carry = (jnp.full((tq, 1), -jnp.inf, F32), jnp.zeros((tq, 1), F32), jnp.zeros((tq, q.shape[1]), F32))
        sl = slice(qi * tq, (qi + 1) * tq)
        carry = attend(q, k_ref[sl, :], v_ref[sl, :], neg_ck[:, sl], carry, True)

        def kv_step(ki, carry, q=q):
            k0 = pl.multiple_of(ki * tq, tq)
            bias = -c_ref[pl.ds(head, 1), pl.ds(k0, tq)]
            return attend(q, k_ref[pl.ds(k0, tq), :], v_ref[pl.ds(k0, tq), :], bias, carry, False)

        if qi > 0:
            carry = lax.fori_loop(0, qi, kv_step, carry)
        _, l, acc = carry
        o_ref[sl, :] = (acc / l).astype(o_ref.dtype)


def _fox_attention(proj, cum, batch, seq, heads, tq):
    dh = FOX_HEAD_DIM
    tq = min(tq, seq)
    kern = functools.partial(_fox_attn_kernel, tq=tq)
    return pl.pallas_call(
        kern,
        grid=(batch, heads),
        in_specs=[
            pl.BlockSpec((seq, dh), lambda b, h: (b, h)),
            pl.BlockSpec((seq, dh), lambda b, h: (b, heads + h)),
            pl.BlockSpec((seq, dh), lambda b, h: (b, 2 * heads + h)),
            pl.BlockSpec((None, heads, seq), lambda b, h: (b, 0, 0)),
        ],
        out_specs=pl.BlockSpec((seq, dh), lambda b, h: (b, h)),
        out_shape=jax.ShapeDtypeStruct((batch * seq, heads * dh), BF16),
        compiler_params=_params("parallel", "parallel"),
        name="fox_attention",
    )(proj, proj, proj, cum)


def _fox_out_kernel(o_ref, gate_ref, w_ref, h_ref, out_ref, og_ref):
    @pl.when(pl.program_id(1) == 0)
    def _():
        og_ref[...] = (o_ref[...].astype(F32) * jax.nn.sigmoid(gate_ref[...].astype(F32))).astype(BF16)

    out_ref[...] = h_ref[...] + jnp.dot(og_ref[...], w_ref[...], preferred_element_type=F32)


def _fox_out(o, proj, w_out, h2, tm, tn):
    n, d = h2.shape
    tm, tn = min(tm, n), min(tn, d)
    gate_block = 3
    return pl.pallas_call(
        _fox_out_kernel,
        grid=(n // tm, d // tn),
        in_specs=[
            pl.BlockSpec((tm, d), lambda i, j: (i, 0)),
            pl.BlockSpec((tm, d), lambda i, j: (i, gate_block)),
            pl.BlockSpec((d, tn), lambda i, j: (0, j)),
            pl.BlockSpec((tm, tn), lambda i, j: (i, j)),
        ],
        out_specs=pl.BlockSpec((tm, tn), lambda i, j: (i, j)),
        out_shape=jax.ShapeDtypeStruct((n, d), F32),
        scratch_shapes=[pltpu.VMEM((tm, d), BF16)],
        compiler_params=_params("parallel", "arbitrary"),
        name="fox_out",
    )(o, proj, w_out, h2)


def _s5_layer(x, norm_gain, w_in, lam_re, lam_im, b_re, b_im, c_re, c_im, d_skip, log_step, w_glu, b_glu, w_out):
    b, s, d = x.shape
    assert b == STATE_BATCH
    ds = w_in.shape[1]
    bmat, cmat, a_re, a_im = _s5_discretise(lam_re, lam_im, b_re, b_im, c_re, c_im, log_step)
    u_tm = _rms_matmul(x, norm_gain, w_in.astype(BF16), jnp.ones((ds,), F32), F32, True, ts=1024, tn=ds)
    y_tm = _s5_scan(u_tm.reshape(s * b, ds), bmat, cmat, a_re, a_im, d_skip.astype(F32), t_chunk=128)
    return _s5_out(y_tm.reshape(s, b * ds), x, w_glu.astype(BF16), b_glu.astype(F32), w_out.astype(BF16), ts=512)


def _fox_layer(h, norm_gain, w_in, b_forget, w_out):
    b, s, d = h.shape
    heads = d // FOX_HEAD_DIM
    w_main = w_in[:, :4 * d].astype(BF16)
    w_forget = jnp.pad(w_in[:, 4 * d:], ((0, 0), (0, LANES - heads))).astype(BF16)
    col_scale = jnp.concatenate([jnp.full((d,), FOX_HEAD_DIM ** -0.5, F32), jnp.ones((3 * d,), F32)])
    proj = _rms_matmul(h, norm_gain, w_main, col_scale, BF16, False, ts=1024, tn=1024)
    f_logit = _rms_matmul(h, norm_gain, w_forget, jnp.ones((LANES,), F32), F32, False, ts=1024, tn=LANES)
    f_logit_t = f_logit[:, :heads].reshape(b, s, heads).transpose(0, 2, 1)
    cum = _forget_cumsum(f_logit_t, b_forget)
    o = _fox_attention(proj, cum, b, s, heads, tq=512)
    return _fox_out(o, proj, w_out.astype(BF16), h.reshape(b * s, d), tm=512, tn=1024).reshape(b, s, d)


def kernel(x, l0_mix_norm, l0_s5_w_in, l0_s5_lambda_re, l0_s5_lambda_im, l0_s5_b_re, l0_s5_b_im, l0_s5_c_re, l0_s5_c_im, l0_s5_d, l0_s5_log_step, l0_s5_w_glu, l0_s5_b_glu, l0_s5_w_out, l0_ffn_norm, l0_moe_w_group, l0_moe_b_group, l0_moe_w_expert, l0_moe_b_expert, l0_moe_w_gate, l0_moe_w_up, l0_moe_w_down, l1_mix_norm, l1_fox_w_in, l1_fox_b_forget, l1_fox_w_out, l1_ffn_norm, l1_moe_w_group, l1_moe_b_group, l1_moe_w_expert, l1_moe_b_expert, l1_moe_w_gate, l1_moe_w_up, l1_moe_w_down, final_norm):
    b, s, d = x.shape
    h = _s5_layer(x, l0_mix_norm, l0_s5_w_in, l0_s5_lambda_re, l0_s5_lambda_im, l0_s5_b_re, l0_s5_b_im,
                  l0_s5_c_re, l0_s5_c_im, l0_s5_d, l0_s5_log_step, l0_s5_w_glu, l0_s5_b_glu, l0_s5_w_out)
    h = _hier_moe(h.reshape(b * s, d), l0_ffn_norm, l0_moe_w_group, l0_moe_b_group, l0_moe_w_expert,
                  l0_moe_b_expert, l0_moe_w_gate, l0_moe_w_up, l0_moe_w_down, None).reshape(b, s, d)
    h = _fox_layer(h, l1_mix_norm, l1_fox_w_in, l1_fox_b_forget, l1_fox_w_out)
    h = _hier_moe(h.reshape(b * s, d), l1_ffn_norm, l1_moe_w_group, l1_moe_b_group, l1_moe_w_expert,
                  l1_moe_b_expert, l1_moe_w_gate, l1_moe_w_up, l1_moe_w_down, final_norm)
    return h.reshape(b, s, d)
```

```python
import functools
import math

import jax
import jax.numpy as jnp
from jax import lax
from jax.experimental import pallas as pl
from jax.experimental.pallas import tpu as pltpu

F32 = jnp.float32
BF16 = jnp.bfloat16

RMS_EPS = 1e-6
STATE_BATCH = 8
SSM_GROUPS_PER_BLOCK = 8
FOX_HEAD_DIM = 128
TOP_K = 2
DISPATCH_BLOCK = 256
LANES = 128
VMEM_LIMIT = 56 * 1024 * 1024


def _params(*sem):
    return pltpu.CompilerParams(dimension_semantics=sem, vmem_limit_bytes=VMEM_LIMIT)


def _rms(x, gain):
    ms = jnp.mean(x * x, axis=-1, keepdims=True)
    return x * lax.rsqrt(ms + RMS_EPS) * gain


def _rms_matmul_kernel(x_ref, g_ref, w_ref, cs_ref, o_ref, xn_ref):
    @pl.when(pl.program_id(2) == 0)
    def _():
        xn_ref[...] = _rms(x_ref[...], g_ref[...]).astype(BF16)

    acc = jnp.dot(xn_ref[...], w_ref[...], preferred_element_type=F32)
    o_ref[...] = (acc * cs_ref[...]).astype(o_ref.dtype)


def _rms_matmul(x, gain, w, col_scale, out_dtype, time_major, ts, tn):
    b, s, d = x.shape
    m = w.shape[1]
    ts, tn = min(ts, s), min(tn, m)
    nj = m // tn
    if time_major:
        out_shape = jax.ShapeDtypeStruct((s, b * m), out_dtype)
        out_spec = pl.BlockSpec((ts, tn), lambda bi, si, j: (si, bi * nj + j))
    else:
        ns = s // ts
        out_shape = jax.ShapeDtypeStruct((b * s, m), out_dtype)
        out_spec = pl.BlockSpec((ts, tn), lambda bi, si, j: (bi * ns + si, j))
    return pl.pallas_call(
        _rms_matmul_kernel,
        grid=(b, s // ts, nj),
        in_specs=[
            pl.BlockSpec((None, ts, d), lambda bi, si, j: (bi, si, 0)),
            pl.BlockSpec((1, d), lambda bi, si, j: (0, 0)),
            pl.BlockSpec((d, tn), lambda bi, si, j: (0, j)),
            pl.BlockSpec((1, tn), lambda bi, si, j: (0, j)),
        ],
        out_specs=out_spec,
        out_shape=out_shape,
        scratch_shapes=[pltpu.VMEM((ts, d), BF16)],
        compiler_params=_params("parallel", "parallel", "arbitrary"),
        name="rms_matmul",
    )(x, gain.reshape(1, d), w, col_scale.reshape(1, m))


def _s5_scan_kernel(u_ref, bmat_ref, cmat_ref, are_ref, aim_ref, d_ref, y_ref, bu_ref, st_ref, *, n_blocks, t_chunk):
    @pl.when(pl.program_id(0) == 0)
    def _():
        st_ref[...] = jnp.zeros_like(st_ref)

    cw = bmat_ref.shape[1]
    sw = bmat_ref.shape[2] // 2
    for gb in range(n_blocks):
        u_blk = u_ref[:, gb * cw:(gb + 1) * cw]
        bu_ref[...] = jnp.dot(u_blk.astype(BF16), bmat_ref[gb], preferred_element_type=F32)
        a_re = jnp.broadcast_to(are_ref[gb], (STATE_BATCH, sw))
        a_im = jnp.broadcast_to(aim_ref[gb], (STATE_BATCH, sw))

        def step(t, carry):
            s_re, s_im = carry
            r0 = pl.multiple_of(t * STATE_BATCH, STATE_BATCH)
            b_re = bu_ref[pl.ds(r0, STATE_BATCH), 0:sw]
            b_im = bu_ref[pl.ds(r0, STATE_BATCH), sw:2 * sw]
            n_re = a_re * s_re - a_im * s_im + b_re
            n_im = a_re * s_im + a_im * s_re + b_im
            bu_ref[pl.ds(r0, STATE_BATCH), 0:sw] = n_re
            bu_ref[pl.ds(r0, STATE_BATCH), sw:2 * sw] = n_im
            return n_re, n_im

        s_re, s_im = lax.fori_loop(0, t_chunk, step, (st_ref[gb, 0], st_ref[gb, 1]), unroll=8)
        st_ref[gb, 0] = s_re
        st_ref[gb, 1] = s_im
        y = jnp.dot(bu_ref[...].astype(BF16), cmat_ref[gb], preferred_element_type=F32)
        y_ref[:, gb * cw:(gb + 1) * cw] = y + d_ref[:, gb * cw:(gb + 1) * cw] * u_blk


def _s5_scan(u_tm, bmat, cmat, a_re, a_im, d_skip, t_chunk):
    rows, ds = u_tm.shape
    seq = rows // STATE_BATCH
    t_chunk = min(t_chunk, seq)
    nb, cw, sw2 = bmat.shape
    tr = t_chunk * STATE_BATCH
    kern = functools.partial(_s5_scan_kernel, n_blocks=nb, t_chunk=t_chunk)
    return pl.pallas_call(
        kern,
        grid=(seq // t_chunk,),
        in_specs=[
            pl.BlockSpec((tr, ds), lambda i: (i, 0)),
            pl.BlockSpec((nb, cw, sw2), lambda i: (0, 0, 0)),
            pl.BlockSpec((nb, sw2, cw), lambda i: (0, 0, 0)),
            pl.BlockSpec((nb, 1, sw2 // 2), lambda i: (0, 0, 0)),
            pl.BlockSpec((nb, 1, sw2 // 2), lambda i: (0, 0, 0)),
            pl.BlockSpec((1, ds), lambda i: (0, 0)),
        ],
        out_specs=pl.BlockSpec((tr, ds), lambda i: (i, 0)),
        out_shape=jax.ShapeDtypeStruct((rows, ds), F32),
        scratch_shapes=[
            pltpu.VMEM((tr, sw2), F32),
            pltpu.VMEM((nb, 2, STATE_BATCH, sw2 // 2), F32),
        ],
        compiler_params=_params("arbitrary"),
        name="s5_scan",
    )(u_tm, bmat, cmat, a_re, a_im, d_skip.reshape(1, ds))


def _s5_discretise(lam_re, lam_im, b_re, b_im, c_re, c_im, log_step):
    lr, li = lam_re.astype(F32), lam_im.astype(F32)
    step = jnp.exp(log_step.astype(F32))[:, None]
    mag = jnp.exp(lr * step)
    ab_re = mag * jnp.cos(li * step)
    ab_im = mag * jnp.sin(li * step)
    den = lr * lr + li * li
    nr, ni = ab_re - 1.0, ab_im
    fr = (nr * lr + ni * li) / den
    fi = (ni * lr - nr * li) / den
    br, bi = b_re.astype(F32), b_im.astype(F32)
    bb_re = fr[..., None] * br - fi[..., None] * bi
    bb_im = fr[..., None] * bi + fi[..., None] * br
    g, n, c = bb_re.shape
    gpb = min(SSM_GROUPS_PER_BLOCK, g)
    nb = g // gpb
    eye = jnp.eye(gpb, dtype=F32)

    def in_blocks(bb):
        return jnp.einsum("bjnc,jk->bjckn", bb.reshape(nb, gpb, n, c), eye).reshape(nb, gpb * c, gpb * n)

    def out_blocks(cc):
        return jnp.einsum("bjcn,jk->bjnkc", cc.reshape(nb, gpb, c, n), eye).reshape(nb, gpb * n, gpb * c)

    bmat = jnp.concatenate([in_blocks(bb_re), in_blocks(bb_im)], axis=-1).astype(BF16)
    cmat = jnp.concatenate([out_blocks(c_re.astype(F32)), -out_blocks(c_im.astype(F32))], axis=1).astype(BF16)
    return bmat, cmat, ab_re.reshape(nb, 1, gpb * n), ab_im.reshape(nb, 1, gpb * n)


def _s5_out_kernel(y_ref, x_ref, wg_ref, bg_ref, wo_ref, o_ref):
    g = jax.nn.gelu(y_ref[...])
    z = jnp.dot(g.astype(BF16), wg_ref[...], preferred_element_type=F32) + bg_ref[...]
    gated = g * jax.nn.sigmoid(z)
    o_ref[...] = x_ref[...] + jnp.dot(gated.astype(BF16), wo_ref[...], preferred_element_type=F32)


def _s5_out(y_tm, x, w_glu, b_glu, w_out, ts):
    b, s, d = x.shape
    ds = w_glu.shape[0]
    ts = min(ts, s)
    return pl.pallas_call(
        _s5_out_kernel,
        grid=(b, s // ts),
        in_specs=[
            pl.BlockSpec((ts, ds), lambda bi, si: (si, bi)),
            pl.BlockSpec((None, ts, d), lambda bi, si: (bi, si, 0)),
            pl.BlockSpec((ds, ds), lambda bi, si: (0, 0)),
            pl.BlockSpec((1, ds), lambda bi, si: (0, 0)),
            pl.BlockSpec((ds, d), lambda bi, si: (0, 0)),
        ],
        out_specs=pl.BlockSpec((None, ts, d), lambda bi, si: (bi, si, 0)),
        out_shape=jax.ShapeDtypeStruct((b, s, d), F32),
        compiler_params=_params("parallel", "parallel"),
        name="s5_out",
    )(y_tm, x, w_glu, b_glu.reshape(1, ds), w_out)


def _pack_halves(x):
    half = x.shape[-1] // 2
    lo = lax.bitcast_convert_type(x[:, :half].astype(BF16).astype(F32), jnp.uint32)
    hi = lax.bitcast_convert_type(x[:, half:].astype(BF16).astype(F32), jnp.uint32)
    return (lo >> 16) | (hi & jnp.uint32(0xFFFF0000))


def _unpack_halves(words):
    lo = lax.bitcast_convert_type(words << 16, F32)
    hi = lax.bitcast_convert_type(words & jnp.uint32(0xFFFF0000), F32)
    return lo, hi


def _store_rows(ref, words):
    rows, nw = words.shape[0], words.shape[1] // LANES
    for c in range(nw):
        ref[pl.ds(c, rows, stride=nw), :] = words[:, c * LANES:(c + 1) * LANES]


def _load_rows(ref, first_row, rows, nw):
    return jnp.concatenate(
        [ref[pl.ds(first_row * nw + c, rows, stride=nw), :] for c in range(nw)], axis=1)


def _router_kernel(h_ref, g_ref, whi_ref, wlo_ref, b_ref, tri_ref, hn_ref, ei_ref, ew_ref, cnt_ref, base_ref,
                   *, n_groups, epg):
    @pl.when(pl.program_id(0) == 0)
    def _():
        base_ref[...] = jnp.zeros_like(base_ref)

    hn = _rms(h_ref[...], g_ref[...])
    _store_rows(hn_ref, _pack_halves(hn))
    hn_hi = hn.astype(BF16)
    hn_lo = (hn - hn_hi.astype(F32)).astype(BF16)
    logits = (jnp.dot(hn_hi, whi_ref[...], preferred_element_type=F32)
              + jnp.dot(hn_lo, whi_ref[...], preferred_element_type=F32)
              + jnp.dot(hn_hi, wlo_ref[...], preferred_element_type=F32)) + b_ref[...]
    lane = lax.broadcasted_iota(jnp.int32, logits.shape, 1)
    n_lanes = logits.shape[1]

    def first_argmax(vals, vmax):
        first = jnp.min(jnp.where(vals == vmax, lane, n_lanes).astype(F32), axis=-1, keepdims=True)
        return first.astype(jnp.int32)

    in_groups = lane < n_groups
    gl = jnp.where(in_groups, logits, -jnp.inf)
    g_max = jnp.max(gl, axis=-1, keepdims=True)
    g_sel = first_argmax(gl, g_max)
    p_group = 1.0 / jnp.sum(jnp.where(in_groups, jnp.exp(logits - g_max), 0.0), axis=-1, keepdims=True)

    lo = n_groups + g_sel * epg
    el = jnp.where(jnp.logical_and(lane >= lo, lane < lo + epg), logits, -jnp.inf)
    t1 = jnp.max(el, axis=-1, keepdims=True)
    i1 = first_argmax(el, t1)
    el2 = jnp.where(lane == i1, -jnp.inf, el)
    t2 = jnp.max(el2, axis=-1, keepdims=True)
    i2 = first_argmax(el2, t2)
    e2x = jnp.exp(t2 - t1)
    w1 = p_group / (1.0 + e2x)
    w2 = p_group * e2x / (1.0 + e2x)
    e1 = i1 - n_groups
    e2 = i2 - n_groups

    onehot = jnp.logical_or(lane == e1, lane == e2)
    earlier = jnp.dot(tri_ref[...], jnp.where(onehot, 1.0, 0.0).astype(BF16), preferred_element_type=F32)
    total = earlier + base_ref[...]
    r1 = jnp.sum(jnp.where(lane == e1, total, 0.0), axis=-1, keepdims=True).astype(jnp.int32)
    r2 = jnp.sum(jnp.where(lane == e2, total, 0.0), axis=-1, keepdims=True).astype(jnp.int32)
    base_ref[...] = base_ref[...] + jnp.sum(jnp.where(onehot, 1.0, 0.0), axis=0, keepdims=True)
    cnt_ref[...] = base_ref[...].astype(jnp.int32)

    ei_ref[...] = jnp.where(lane == 0, e1, jnp.where(lane == 1, e2, jnp.where(lane == 2, r1, jnp.where(lane == 3, r2, 0))))
    ew_ref[...] = jnp.where(lane == 0, w1, jnp.where(lane == 1, w2, 0.0))


def _router(h2, gain, w_router, b_router, n_groups, epg, tm):
    n, d = h2.shape
    tm = min(tm, n)
    wc = w_router.shape[1]
    nw = d // (2 * LANES)
    w_hi = w_router.astype(BF16)
    w_lo = (w_router - w_hi.astype(F32)).astype(BF16)
    tri = jnp.tri(tm, k=-1, dtype=BF16)
    kern = functools.partial(_router_kernel, n_groups=n_groups, epg=epg)
    return pl.pallas_call(
        kern,
        grid=(n // tm,),
        in_specs=[
            pl.BlockSpec((tm, d), lambda i: (i, 0)),
            pl.BlockSpec((1, d), lambda i: (0, 0)),
            pl.BlockSpec((d, wc), lambda i: (0, 0)),
            pl.BlockSpec((d, wc), lambda i: (0, 0)),
            pl.BlockSpec((1, wc), lambda i: (0, 0)),
            pl.BlockSpec((tm, tm), lambda i: (0, 0)),
        ],
        out_specs=[
            pl.BlockSpec((tm * nw, LANES), lambda i: (i, 0)),
            pl.BlockSpec((tm, wc), lambda i: (i, 0)),
            pl.BlockSpec((tm, wc), lambda i: (i, 0)),
            pl.BlockSpec((1, wc), lambda i: (0, 0)),
        ],
        out_shape=[
            jax.ShapeDtypeStruct((n * nw, LANES), jnp.uint32),
            jax.ShapeDtypeStruct((n, wc), jnp.int32),
            jax.ShapeDtypeStruct((n, wc), F32),
            jax.ShapeDtypeStruct((1, wc), jnp.int32),
        ],
        scratch_shapes=[pltpu.VMEM((1, wc), F32)],
        compiler_params=_params("arbitrary"),
        name="moe_router",
    )(h2, gain.reshape(1, d), w_hi, w_lo, b_router, tri)


def _dispatch_kernel(dest_ref, last_ref, nv_ref, hn_ref, x_ref, zero_ref, zsem, tsem, rsem,
                     *, n_experts, n_blocks, tb, nw):
    i = pl.program_id(0)
    tm = hn_ref.shape[0] // nw

    def row(ref, r):
        return ref.at[pl.ds(pl.multiple_of(r * nw, nw), nw), :]

    def zero_copy(block, sem):
        line0 = pl.multiple_of(jnp.maximum(block, 0) * (tb * nw), tb * nw)
        return pltpu.make_async_copy(zero_ref, x_ref.at[pl.ds(line0, tb * nw), :], sem)

    def for_expert_tails(fn):
        def body(e, c):
            @pl.when(last_ref[e] >= 0)
            def _():
                fn(zero_copy(last_ref[e], zsem.at[e]))
            return c
        lax.fori_loop(0, n_experts, body, 0)

    def for_unused_blocks(fn):
        def body(b, c):
            @pl.when(b >= nv_ref[0])
            def _():
                fn(zero_copy(b, tsem))
            return c
        lax.fori_loop(0, n_blocks, body, 0)

    @pl.when(i == 0)
    def _():
        zero_ref[...] = jnp.zeros_like(zero_ref)
        for_expert_tails(lambda cp: cp.start())
        for_unused_blocks(lambda cp: cp.start())
        for_expert_tails(lambda cp: cp.wait())
        for_unused_blocks(lambda cp: cp.wait())

    def start_one(j, c):
        for k in range(TOP_K):
            dst = dest_ref[(i * tm + j) * TOP_K + k]
            pltpu.make_async_copy(row(hn_ref, j), row(x_ref, dst), rsem).start()
        return c

    def wait_one(j, c):
        pltpu.make_async_copy(row(hn_ref, 0), row(x_ref, 0), rsem).wait()
        return c

    lax.fori_loop(0, tm, start_one, 0, unroll=4)
    lax.fori_loop(0, TOP_K * tm, wait_one, 0, unroll=8)


def _dispatch(hn_rows, dest, last_block, n_valid, n_blocks, nw, tm):
    n = hn_rows.shape[0] // nw
    tm = min(tm, n)
    n_experts = last_block.shape[0]
    tb = DISPATCH_BLOCK
    kern = functools.partial(_dispatch_kernel, n_experts=n_experts, n_blocks=n_blocks, tb=tb, nw=nw)
    grid_spec = pltpu.PrefetchScalarGridSpec(
        num_scalar_prefetch=3,
        grid=(n // tm,),
        in_specs=[pl.BlockSpec((tm * nw, LANES), lambda i, dest, last, nv: (i, 0))],
        out_specs=pl.BlockSpec(memory_space=pl.ANY),
        scratch_shapes=[
            pltpu.VMEM((tb * nw, LANES), jnp.uint32),
            pltpu.SemaphoreType.DMA((n_experts,)),
            pltpu.SemaphoreType.DMA,
            pltpu.SemaphoreType.DMA,
        ],
    )
    return pl.pallas_call(
        kern,
        grid_spec=grid_spec,
        out_shape=jax.ShapeDtypeStruct((n_blocks * tb * nw, LANES), jnp.uint32),
        compiler_params=_params("arbitrary"),
        name="moe_dispatch",
    )(dest, last_block, n_valid, hn_rows)


def _expert_kernel(be_ref, nv_ref, seg_ref, nxt_ref, x_ref, wg_hbm, wu_hbm, wd_hbm, o_ref,
                   wg_buf, wu_buf, wd_buf, wgb_ref, wub_ref, wdb_ref, sem):
    i = pl.program_id(0)
    valid = i < nv_ref[0]
    new_expert = jnp.logical_or(i == 0, be_ref[i] != be_ref[jnp.maximum(i - 1, 0)])
    slot = seg_ref[i] % 2

    def weight_copies(e, s):
        return (pltpu.make_async_copy(wg_hbm.at[e], wg_buf.at[s], sem.at[s, 0]),
                pltpu.make_async_copy(wu_hbm.at[e], wu_buf.at[s], sem.at[s, 1]),
                pltpu.make_async_copy(wd_hbm.at[e], wd_buf.at[s], sem.at[s, 2]))

    @pl.when(i == 0)
    def _():
        for cp in weight_copies(be_ref[0], 0):
            cp.start()

    @pl.when(jnp.logical_and(valid, new_expert))
    def _():
        @pl.when(nxt_ref[i] >= 0)
        def _():
            for cp in weight_copies(nxt_ref[i], 1 - slot):
                cp.start()

        for cp in weight_copies(be_ref[i], slot):
            cp.wait()
        wgb_ref[...] = wg_buf[slot].astype(BF16)
        wub_ref[...] = wu_buf[slot].astype(BF16)
        wdb_ref[...] = wd_buf[slot].astype(BF16)

    @pl.when(valid)
    def _():
        half = wgb_ref.shape[0] // 2
        nw = half // LANES
        x_lo, x_hi = _unpack_halves(_load_rows(x_ref, 0, x_ref.shape[0] // nw, nw))
        x_lo, x_hi = x_lo.astype(BF16), x_hi.astype(BF16)

        def proj(w_ref):
            return (jnp.dot(x_lo, w_ref[:half, :], preferred_element_type=F32)
                    + jnp.dot(x_hi, w_ref[half:, :], preferred_element_type=F32))

        hid = jax.nn.silu(proj(wgb_ref)) * proj(wub_ref)
        _store_rows(o_ref, _pack_halves(jnp.dot(hid.astype(BF16), wdb_ref[...], preferred_element_type=F32)))

    @pl.when(jnp.logical_not(valid))
    def _():
        o_ref[...] = jnp.zeros_like(o_ref)


def _experts(x_disp, block_expert, n_valid, block_segment, next_expert, w_gate, w_up, w_down):
    d, de = w_gate.shape[1], w_gate.shape[2]
    nw = d // (2 * LANES)
    rows = x_disp.shape[0] // nw
    tb = DISPATCH_BLOCK
    grid_spec = pltpu.PrefetchScalarGridSpec(
        num_scalar_prefetch=4,
        grid=(rows // tb,),
        in_specs=[
            pl.BlockSpec((tb * nw, LANES),
                         lambda i, be, nv, seg, nxt: (jnp.maximum(jnp.minimum(i, nv[0] - 1), 0), 0)),
            pl.BlockSpec(memory_space=pl.ANY),
            pl.BlockSpec(memory_space=pl.ANY),
            pl.BlockSpec(memory_space=pl.ANY),
        ],
        out_specs=pl.BlockSpec((tb * nw, LANES), lambda i, be, nv, seg, nxt: (i, 0)),
        scratch_shapes=[
            pltpu.VMEM((2, d, de), w_gate.dtype),
            pltpu.VMEM((2, d, de), w_up.dtype),
            pltpu.VMEM((2, de, d), w_down.dtype),
            pltpu.VMEM((d, de), BF16),
            pltpu.VMEM((d, de), BF16),
            pltpu.VMEM((de, d), BF16),
            pltpu.SemaphoreType.DMA((2, 3)),
        ],
    )
    return pl.pallas_call(
        _expert_kernel,
        grid_spec=grid_spec,
        out_shape=jax.ShapeDtypeStruct((rows * nw, LANES), jnp.uint32),
        compiler_params=_params("arbitrary"),
        name="moe_experts",
    )(block_expert, n_valid, block_segment, next_expert, x_disp, w_gate, w_up, w_down)


def _combine_kernel(dest_ref, h_ref, ew_ref, g_ref, y_ref, o_ref, buf_ref, sem, *, final_norm):
    tm, d = h_ref.shape
    nw = d // (2 * LANES)
    i = pl.program_id(0)
    n_tiles = pl.num_programs(0)

    def row(ref, r):
        return ref.at[pl.ds(pl.multiple_of(r * nw, nw), nw), :]

    def start_tile(tile, slot):
        def start_one(j, c):
            for k in range(TOP_K):
                src = dest_ref[(tile * tm + j) * TOP_K + k]
                pltpu.make_async_copy(row(y_ref, src), row(buf_ref.at[slot], k * tm + j), sem.at[slot]).start()
            return c

        lax.fori_loop(0, tm, start_one, 0, unroll=4)

    def wait_tile(slot):
        def wait_one(j, c):
            pltpu.make_async_copy(row(y_ref, 0), row(buf_ref.at[slot], 0), sem.at[slot]).wait()
            return c

        lax.fori_loop(0, TOP_K * tm, wait_one, 0, unroll=8)

    slot = i % 2

    @pl.when(i == 0)
    def _():
        start_tile(0, 0)

    @pl.when(i + 1 < n_tiles)
    def _():
        start_tile(i + 1, 1 - slot)

    wait_tile(slot)
    h = h_ref[...]
    w = ew_ref[...]
    lo, hi = h[:, :d // 2], h[:, d // 2:]
    for k in range(TOP_K):
        y_lo, y_hi = _unpack_halves(_load_rows(buf_ref.at[slot], k * tm, tm, nw))
        lo = lo + y_lo * w[:, k:k + 1]
        hi = hi + y_hi * w[:, k:k + 1]
    if final_norm:
        ms = (jnp.sum(lo * lo, axis=-1, keepdims=True) + jnp.sum(hi * hi, axis=-1, keepdims=True)) / d
        scale = lax.rsqrt(ms + RMS_EPS)
        lo = lo * scale * g_ref[:, :d // 2]
        hi = hi * scale * g_ref[:, d // 2:]
    o_ref[:, :d // 2] = lo
    o_ref[:, d // 2:] = hi


def _combine(h2, y_disp, dest, ew, gain, final_norm, tm):
    n, d = h2.shape
    tm = min(tm, n)
    nw = d // (2 * LANES)
    wc = ew.shape[1]
    kern = functools.partial(_combine_kernel, final_norm=final_norm)
    grid_spec = pltpu.PrefetchScalarGridSpec(
        num_scalar_prefetch=1,
        grid=(n // tm,),
        in_specs=[
            pl.BlockSpec((tm, d), lambda i, dest: (i, 0)),
            pl.BlockSpec((tm, wc), lambda i, dest: (i, 0)),
            pl.BlockSpec((1, d), lambda i, dest: (0, 0)),
            pl.BlockSpec(memory_space=pl.ANY),
        ],
        out_specs=pl.BlockSpec((tm, d), lambda i, dest: (i, 0)),
        scratch_shapes=[
            pltpu.VMEM((2, TOP_K * tm * nw, LANES), jnp.uint32),
            pltpu.SemaphoreType.DMA((2,)),
        ],
    )
    return pl.pallas_call(
        kern,
        grid_spec=grid_spec,
        out_shape=jax.ShapeDtypeStruct((n, d), F32),
        compiler_params=_params("arbitrary"),
        name="moe_combine",
    )(dest, h2, ew, gain.reshape(1, d), y_disp)


def _hier_moe(h2, norm_gain, w_group, b_group, w_expert, b_expert, w_gate, w_up, w_down, final_gain):
    n, d = h2.shape
    n_groups = w_group.shape[1]
    n_experts = w_expert.shape[1]
    epg = n_experts // n_groups
    n_logits = n_groups + n_experts
    wc = -(-n_logits // LANES) * LANES
    w_router = jnp.concatenate(
        [w_group.astype(F32), w_expert.astype(F32), jnp.zeros((d, wc - n_logits), F32)], axis=1)
    b_router = jnp.concatenate(
        [b_group.astype(F32), b_expert.astype(F32), jnp.zeros((wc - n_logits,), F32)]).reshape(1, wc)
    hn_packed, ei, ew, counts = _router(h2, norm_gain, w_router, b_router, n_groups, epg, tm=512)

    tb = DISPATCH_BLOCK
    n_pairs = n * TOP_K
    counts = counts[0, :n_experts]
    blocks_per_expert = (counts + tb - 1) // tb
    block_end = jnp.cumsum(blocks_per_expert)
    block_start = block_end - blocks_per_expert
    expert_id, rank = ei[:, :TOP_K], ei[:, TOP_K:2 * TOP_K]
    dest = (jnp.take(block_start, expert_id) * tb + rank).reshape(-1).astype(jnp.int32)
    last_block = jnp.where(blocks_per_expert > 0, block_end - 1, -1).astype(jnp.int32)
    n_blocks = -(-n_pairs // tb) + n_experts
    n_valid = block_end[-1:].astype(jnp.int32)
    blk = jnp.minimum(jnp.arange(n_blocks, dtype=jnp.int32), n_valid[0] - 1)
    block_expert = jnp.searchsorted(block_end, blk, side="right").astype(jnp.int32)
    expert_range = jnp.arange(n_experts, dtype=jnp.int32)
    owners = jnp.where(blocks_per_expert > 0, expert_range, n_experts)
    owner_at_or_after = lax.cummin(owners, axis=0, reverse=True)
    owner_after = jnp.concatenate([owner_at_or_after[1:], jnp.full((1,), n_experts, jnp.int32)])
    next_expert = jnp.take(jnp.where(owner_after < n_experts, owner_after, -1), block_expert).astype(jnp.int32)
    changed = jnp.concatenate([jnp.zeros((1,), jnp.int32),
                               (block_expert[1:] != block_expert[:-1]).astype(jnp.int32)])
    block_segment = jnp.cumsum(changed).astype(jnp.int32)

    x_disp = _dispatch(hn_packed, dest, last_block, n_valid, n_blocks, d // (2 * LANES), tm=512)
    y_disp = _experts(x_disp, block_expert, n_valid, block_segment, next_expert, w_gate, w_up, w_down)
    gain = norm_gain if final_gain is None else final_gain
    return _combine(h2, y_disp, dest, ew, gain, final_gain is not None, tm=512)


def _forget_kernel(f_ref, b_ref, c_ref):
    x = jax.nn.log_sigmoid(f_ref[...] + b_ref[...])
    s = x.shape[1]
    lane = lax.broadcasted_iota(jnp.int32, x.shape, 1)
    shift = 1
    while shift < s:
        x = x + jnp.where(lane >= shift, pltpu.roll(x, shift, axis=1), 0.0)
        shift *= 2
    c_ref[...] = x


def _forget_cumsum(f_logit_t, b_forget):
    b, h, s = f_logit_t.shape
    return pl.pallas_call(
        _forget_kernel,
        grid=(b,),
        in_specs=[
            pl.BlockSpec((None, h, s), lambda i: (i, 0, 0)),
            pl.BlockSpec((h, 1), lambda i: (0, 0)),
        ],
        out_specs=pl.BlockSpec((None, h, s), lambda i: (i, 0, 0)),
        out_shape=jax.ShapeDtypeStruct((b, h, s), F32),
        compiler_params=_params("parallel"),
        name="fox_forget",
    )(f_logit_t, b_forget.astype(F32).reshape(h, 1))


def _fox_attn_kernel(q_ref, k_ref, v_ref, c_ref, o_ref, *, tk, n_sub):
    seq, dh = q_ref.shape
    head = pl.program_id(1)
    log2e = math.log2(math.e)
    row = lax.broadcasted_iota(jnp.int32, (tk, tk), 0)
    col = lax.broadcasted_iota(jnp.int32, (tk, tk), 1)
    causal = col <= row

    def attend(q, k, v, bias, carry, mask):
        m, l, acc = carry
        s = lax.dot_general(q, k, (((1,), (1,)), ((), ())), preferred_element_type=F32) + bias
        if mask:
            s = jnp.where(causal, s, -jnp.inf)
        m_new = jnp.maximum(m, jnp.max(s, axis=-1, keepdims=True))
        alpha = jnp.exp2(m - m_new)
        p = jnp.exp2(s - m_new)
        l = alpha * l + jnp.sum(p, axis=-1, keepdims=True)
        acc = alpha * acc + jnp.dot(p.astype(BF16), v, preferred_element_type=F32)
        return m_new, l, acc

    def key_block(k0):
        bias = c_ref[pl.ds(head, 1), pl.ds(k0, tk)] * (-log2e)
        return k_ref[pl.ds(k0, tk), :], v_ref[pl.ds(k0, tk), :], bias

    for qt in range(seq // (tk * n_sub)):
        first = qt * n_sub
        qs = [q_ref[(first + a) * tk:(first + a + 1) * tk, :] for a in range(n_sub)]
        init = (jnp.full((tk, 1), -jnp.inf, F32), jnp.zeros((tk, 1), F32), jnp.zeros((tk, dh), F32))
        carries = (init,) * n_sub

        def kv_step(ki, carries, qs=qs):
            k, v, bias = key_block(pl.multiple_of(ki * tk, tk))
            return tuple(attend(qs[a], k, v, bias, carries[a], False) for a in range(n_sub))

        if qt > 0:
            carries = lax.fori_loop(0, first, kv_step, carries)
        carries = list(carries)
        for kb in range(n_sub):
            k, v, bias = key_block((first + kb) * tk)
            for a in range(kb, n_sub):
                carries[a] = attend(qs[a], k, v, bias, carries[a], a == kb)
        for a in range(n_sub):
            _, l, acc = carries[a]
            o_ref[(first + a) * tk:(first + a + 1) * tk, :] = (acc / l).astype(o_ref.dtype)


def _fox_attention(proj, cum, batch, seq, heads, tk, n_sub):
    dh = FOX_HEAD_DIM
    tk = min(tk, seq)
    n_sub = min(n_sub, seq // tk)
    kern = functools.partial(_fox_attn_kernel, tk=tk, n_sub=n_sub)
    return pl.pallas_call(
        kern,
        grid=(batch, heads),
        in_specs=[
            pl.BlockSpec((seq, dh), lambda b, h: (b, h)),
            pl.BlockSpec((seq, dh), lambda b, h: (b, heads + h)),
            pl.BlockSpec((seq, dh), lambda b, h: (b, 2 * heads + h)),
            pl.BlockSpec((None, heads, seq), lambda b, h: (b, 0, 0)),
        ],
        out_specs=pl.BlockSpec((seq, dh), lambda b, h: (b, h)),
        out_shape=jax.ShapeDtypeStruct((batch * seq, heads * dh), BF16),
        compiler_params=_params("parallel", "parallel"),
        name="fox_attention",
    )(proj, proj, proj, cum)


def _fox_out_kernel(o_ref, gate_ref, w_ref, h_ref, out_ref):
    gated = (o_ref[...].astype(F32) * jax.nn.sigmoid(gate_ref[...].astype(F32))).astype(BF16)
    out_ref[...] = h_ref[...] + jnp.dot(gated, w_ref[...], preferred_element_type=F32)


def _fox_out(o, proj, w_out, h2, tm):
    n, d = h2.shape
    tm = min(tm, n)
    gate_block = 3
    return pl.pallas_call(
        _fox_out_kernel,
        grid=(n // tm,),
        in_specs=[
            pl.BlockSpec((tm, d), lambda i: (i, 0)),
            pl.BlockSpec((tm, d), lambda i: (i, gate_block)),
            pl.BlockSpec((d, d), lambda i: (0, 0)),
            pl.BlockSpec((tm, d), lambda i: (i, 0)),
        ],
        out_specs=pl.BlockSpec((tm, d), lambda i: (i, 0)),
        out_shape=jax.ShapeDtypeStruct((n, d), F32),
        compiler_params=_params("parallel"),
        name="fox_out",
    )(o, proj, w_out, h2)


def _s5_layer(x, norm_gain, w_in, lam_re, lam_im, b_re, b_im, c_re, c_im, d_skip, log_step, w_glu, b_glu, w_out):
    b, s, d = x.shape
    assert b == STATE_BATCH
    ds = w_in.shape[1]
    bmat, cmat, a_re, a_im = _s5_discretise(lam_re, lam_im, b_re, b_im, c_re, c_im, log_step)
    u_tm = _rms_matmul(x, norm_gain, w_in.astype(BF16), jnp.ones((ds,), F32), F32, True, ts=1024, tn=ds)
    y_tm = _s5_scan(u_tm.reshape(s * b, ds), bmat, cmat, a_re, a_im, d_skip.astype(F32), t_chunk=128)
    return _s5_out(y_tm.reshape(s, b * ds), x, w_glu.astype(BF16), b_glu.astype(F32), w_out.astype(BF16), ts=512)


def _fox_layer(h, norm_gain, w_in, b_forget, w_out):
    b, s, d = h.shape
    heads = d // FOX_HEAD_DIM
    w_main = w_in[:, :4 * d].astype(BF16)
    w_forget = jnp.pad(w_in[:, 4 * d:], ((0, 0), (0, LANES - heads))).astype(BF16)
    q_scale = FOX_HEAD_DIM ** -0.5 * math.log2(math.e)
    col_scale = jnp.concatenate([jnp.full((d,), q_scale, F32), jnp.ones((3 * d,), F32)])
    proj = _rms_matmul(h, norm_gain, w_main, col_scale, BF16, False, ts=1024, tn=1024)
    f_logit = _rms_matmul(h, norm_gain, w_forget, jnp.ones((LANES,), F32), F32, False, ts=1024, tn=LANES)
    f_logit_t = f_logit[:, :heads].reshape(b, s, heads).transpose(0, 2, 1)
    cum = _forget_cumsum(f_logit_t, b_forget)
    o = _fox_attention(proj, cum, b, s, heads, tk=512, n_sub=2)
    return _fox_out(o, proj, w_out.astype(BF16), h.reshape(b * s, d), tm=512).reshape(b, s, d)


def kernel(x, l0_mix_norm, l0_s5_w_in, l0_s5_lambda_re, l0_s5_lambda_im, l0_s5_b_re, l0_s5_b_im, l0_s5_c_re, l0_s5_c_im, l0_s5_d, l0_s5_log_step, l0_s5_w_glu, l0_s5_b_glu, l0_s5_w_out, l0_ffn_norm, l0_moe_w_group, l0_moe_b_group, l0_moe_w_expert, l0_moe_b_expert, l0_moe_w_gate, l0_moe_w_up, l0_moe_w_down, l1_mix_norm, l1_fox_w_in, l1_fox_b_forget, l1_fox_w_out, l1_ffn_norm, l1_moe_w_group, l1_moe_b_group, l1_moe_w_expert, l1_moe_b_expert, l1_moe_w_gate, l1_moe_w_up, l1_moe_w_down, final_norm):
    b, s, d = x.shape
    h = _s5_layer(x, l0_mix_norm, l0_s5_w_in, l0_s5_lambda_re, l0_s5_lambda_im, l0_s5_b_re, l0_s5_b_im,
                  l0_s5_c_re, l0_s5_c_im, l0_s5_d, l0_s5_log_step, l0_s5_w_glu, l0_s5_b_glu, l0_s5_w_out)
    h = _hier_moe(h.reshape(b * s, d), l0_ffn_norm, l0_moe_w_group, l0_moe_b_group, l0_moe_w_expert,
                  l0_moe_b_expert, l0_moe_w_gate, l0_moe_w_up, l0_moe_w_down, None).reshape(b, s, d)
    h = _fox_layer(h, l1_mix_norm, l1_fox_w_in, l1_fox_b_forget, l1_fox_w_out)
    h = _hier_moe(h.reshape(b * s, d), l1_ffn_norm, l1_moe_w_group, l1_moe_b_group, l1_moe_w_expert,
                  l1_moe_b_expert, l1_moe_w_gate, l1_moe_w_up, l1_moe_w_down, final_norm)
    return h.reshape(b, s, d)
```

```python
import functools
import math

import jax
import jax.numpy as jnp
from jax import lax
from jax.experimental import pallas as pl
from jax.experimental.pallas import tpu as pltpu

F32 = jnp.float32
BF16 = jnp.bfloat16

RMS_EPS = 1e-6
STATE_BATCH = 8
SSM_GROUPS_PER_BLOCK = 8
FOX_HEAD_DIM = 128
TOP_K = 2
DISPATCH_BLOCK = 256
LANES = 128
VMEM_LIMIT = 56 * 1024 * 1024


def _params(*sem):
    return pltpu.CompilerParams(dimension_semantics=sem, vmem_limit_bytes=VMEM_LIMIT)


def _rms(x, gain):
    ms = jnp.mean(x * x, axis=-1, keepdims=True)
    return x * lax.rsqrt(ms + RMS_EPS) * gain


def _rms_matmul_kernel(*refs, has_side):
    if has_side:
        x_ref, g_ref, w_ref, cs_ref, ws_ref, o_ref, side_ref, xn_ref = refs
    else:
        x_ref, g_ref, w_ref, cs_ref, o_ref, xn_ref = refs

    @pl.when(pl.program_id(2) == 0)
    def _():
        xn_ref[...] = _rms(x_ref[...], g_ref[...]).astype(BF16)
        if has_side:
            side_ref[...] = jnp.dot(xn_ref[...], ws_ref[...], preferred_element_type=F32)

    acc = jnp.dot(xn_ref[...], w_ref[...], preferred_element_type=F32)
    o_ref[...] = (acc * cs_ref[...]).astype(o_ref.dtype)


def _rms_matmul(x, gain, w, col_scale, out_dtype, time_major, ts, tn, w_side=None):
    b, s, d = x.shape
    m = w.shape[1]
    ts, tn = min(ts, s), min(tn, m)
    nj = m // tn
    ns = s // ts
    if time_major:
        out_shape = [jax.ShapeDtypeStruct((s, b * m), out_dtype)]
        out_specs = [pl.BlockSpec((ts, tn), lambda bi, si, j: (si, bi * nj + j))]
    else:
        out_shape = [jax.ShapeDtypeStruct((b * s, m), out_dtype)]
        out_specs = [pl.BlockSpec((ts, tn), lambda bi, si, j: (bi * ns + si, j))]
    in_specs = [
        pl.BlockSpec((None, ts, d), lambda bi, si, j: (bi, si, 0)),
        pl.BlockSpec((1, d), lambda bi, si, j: (0, 0)),
        pl.BlockSpec((d, tn), lambda bi, si, j: (0, j)),
        pl.BlockSpec((1, tn), lambda bi, si, j: (0, j)),
    ]
    args = [x, gain.reshape(1, d), w, col_scale.reshape(1, m)]
    if w_side is not None:
        ms = w_side.shape[1]
        in_specs.append(pl.BlockSpec((d, ms), lambda bi, si, j: (0, 0)))
        args.append(w_side)
        out_shape.append(jax.ShapeDtypeStruct((b * s, ms), F32))
        out_specs.append(pl.BlockSpec((ts, ms), lambda bi, si, j: (bi * ns + si, 0)))
    outs = pl.pallas_call(
        functools.partial(_rms_matmul_kernel, has_side=w_side is not None),
        grid=(b, ns, nj),
        in_specs=in_specs,
        out_specs=out_specs,
        out_shape=out_shape,
        scratch_shapes=[pltpu.VMEM((ts, d), BF16)],
        compiler_params=_params("parallel", "parallel", "arbitrary"),
        name="rms_matmul",
    )(*args)
    return outs[0] if w_side is None else outs


def _s5_scan_kernel(u_ref, bmat_ref, cmat_ref, are_ref, aim_ref, d_ref, y_ref, bu2_ref, st_ref, *, n_blocks, t_chunk):
    @pl.when(pl.program_id(0) == 0)
    def _():
        st_ref[...] = jnp.zeros_like(st_ref)

    cw = bmat_ref.shape[1]
    sw = bmat_ref.shape[2] // 2
    for gb in range(n_blocks):
        u_blk = u_ref[:, gb * cw:(gb + 1) * cw]
        bu_ref = bu2_ref.at[gb % 2]
        bu_ref[...] = jnp.dot(u_blk.astype(BF16), bmat_ref[gb], preferred_element_type=F32)
        a_re = jnp.broadcast_to(are_ref[gb], (STATE_BATCH, sw))
        a_im = jnp.broadcast_to(aim_ref[gb], (STATE_BATCH, sw))

        def step(t, carry):
            s_re, s_im = carry
            r0 = pl.multiple_of(t * STATE_BATCH, STATE_BATCH)
            b_re = bu_ref[pl.ds(r0, STATE_BATCH), 0:sw]
            b_im = bu_ref[pl.ds(r0, STATE_BATCH), sw:2 * sw]
            n_re = a_re * s_re - a_im * s_im + b_re
            n_im = a_re * s_im + a_im * s_re + b_im
            bu_ref[pl.ds(r0, STATE_BATCH), 0:sw] = n_re
            bu_ref[pl.ds(r0, STATE_BATCH), sw:2 * sw] = n_im
            return n_re, n_im

        s_re, s_im = lax.fori_loop(0, t_chunk, step, (st_ref[gb, 0], st_ref[gb, 1]), unroll=True)
        st_ref[gb, 0] = s_re
        st_ref[gb, 1] = s_im
        y = jnp.dot(bu_ref[...].astype(BF16), cmat_ref[gb], preferred_element_type=F32)
        y_ref[:, gb * cw:(gb + 1) * cw] = y + d_ref[:, gb * cw:(gb + 1) * cw] * u_blk


def _s5_scan(u_tm, bmat, cmat, a_re, a_im, d_skip, t_chunk):
    rows, ds = u_tm.shape
    seq = rows // STATE_BATCH
    t_chunk = min(t_chunk, seq)
    nb, cw, sw2 = bmat.shape
    tr = t_chunk * STATE_BATCH
    kern = functools.partial(_s5_scan_kernel, n_blocks=nb, t_chunk=t_chunk)
    return pl.pallas_call(
        kern,
        grid=(seq // t_chunk,),
        in_specs=[
            pl.BlockSpec((tr, ds), lambda i: (i, 0)),
            pl.BlockSpec((nb, cw, sw2), lambda i: (0, 0, 0)),
            pl.BlockSpec((nb, sw2, cw), lambda i: (0, 0, 0)),
            pl.BlockSpec((nb, 1, sw2 // 2), lambda i: (0, 0, 0)),
            pl.BlockSpec((nb, 1, sw2 // 2), lambda i: (0, 0, 0)),
            pl.BlockSpec((1, ds), lambda i: (0, 0)),
        ],
        out_specs=pl.BlockSpec((tr, ds), lambda i: (i, 0)),
        out_shape=jax.ShapeDtypeStruct((rows, ds), F32),
        scratch_shapes=[
            pltpu.VMEM((2, tr, sw2), F32),
            pltpu.VMEM((nb, 2, STATE_BATCH, sw2 // 2), F32),
        ],
        compiler_params=_params("arbitrary"),
        name="s5_scan",
    )(u_tm, bmat, cmat, a_re, a_im, d_skip.reshape(1, ds))


def _s5_discretise(lam_re, lam_im, b_re, b_im, c_re, c_im, log_step):
    lr, li = lam_re.astype(F32), lam_im.astype(F32)
    step = jnp.exp(log_step.astype(F32))[:, None]
    mag = jnp.exp(lr * step)
    ab_re = mag * jnp.cos(li * step)
    ab_im = mag * jnp.sin(li * step)
    den = lr * lr + li * li
    nr, ni = ab_re - 1.0, ab_im
    fr = (nr * lr + ni * li) / den
    fi = (ni * lr - nr * li) / den
    br, bi = b_re.astype(F32), b_im.astype(F32)
    bb_re = fr[..., None] * br - fi[..., None] * bi
    bb_im = fr[..., None] * bi + fi[..., None] * br
    g, n, c = bb_re.shape
    gpb = min(SSM_GROUPS_PER_BLOCK, g)
    nb = g // gpb
    eye = jnp.eye(gpb, dtype=F32)

    def in_blocks(bb):
        return jnp.einsum("bjnc,jk->bjckn", bb.reshape(nb, gpb, n, c), eye).reshape(nb, gpb * c, gpb * n)

    def out_blocks(cc):
        return jnp.einsum("bjcn,jk->bjnkc", cc.reshape(nb, gpb, c, n), eye).reshape(nb, gpb * n, gpb * c)

    bmat = jnp.concatenate([in_blocks(bb_re), in_blocks(bb_im)], axis=-1).astype(BF16)
    cmat = jnp.concatenate([out_blocks(c_re.astype(F32)), -out_blocks(c_im.astype(F32))], axis=1).astype(BF16)
    return bmat, cmat, ab_re.reshape(nb, 1, gpb * n), ab_im.reshape(nb, 1, gpb * n)


def _s5_out_kernel(y_ref, x_ref, wg_ref, bg_ref, wo_ref, o_ref):
    g = jax.nn.gelu(y_ref[...])
    z = jnp.dot(g.astype(BF16), wg_ref[...], preferred_element_type=F32) + bg_ref[...]
    gated = g * jax.nn.sigmoid(z)
    o_ref[...] = x_ref[...] + jnp.dot(gated.astype(BF16), wo_ref[...], preferred_element_type=F32)


def _s5_out(y_tm, x, w_glu, b_glu, w_out, ts):
    b, s, d = x.shape
    ds = w_glu.shape[0]
    ts = min(ts, s)
    return pl.pallas_call(
        _s5_out_kernel,
        grid=(b, s // ts),
        in_specs=[
            pl.BlockSpec((ts, ds), lambda bi, si: (si, bi)),
            pl.BlockSpec((None, ts, d), lambda bi, si: (bi, si, 0)),
            pl.BlockSpec((ds, ds), lambda bi, si: (0, 0)),
            pl.BlockSpec((1, ds), lambda bi, si: (0, 0)),
            pl.BlockSpec((ds, d), lambda bi, si: (0, 0)),
        ],
        out_specs=pl.BlockSpec((None, ts, d), lambda bi, si: (bi, si, 0)),
        out_shape=jax.ShapeDtypeStruct((b, s, d), F32),
        compiler_params=_params("parallel", "parallel"),
        name="s5_out",
    )(y_tm, x, w_glu, b_glu.reshape(1, ds), w_out)


def _pack_halves(x):
    half = x.shape[-1] // 2
    lo = lax.bitcast_convert_type(x[:, :half].astype(BF16).astype(F32), jnp.uint32)
    hi = lax.bitcast_convert_type(x[:, half:].astype(BF16).astype(F32), jnp.uint32)
    return (lo >> 16) | (hi & jnp.uint32(0xFFFF0000))


def _unpack_halves(words):
    lo = lax.bitcast_convert_type(words << 16, F32)
    hi = lax.bitcast_convert_type(words & jnp.uint32(0xFFFF0000), F32)
    return lo, hi


def _store_rows(ref, words):
    rows, nw = words.shape[0], words.shape[1] // LANES
    for c in range(nw):
        ref[pl.ds(c, rows, stride=nw), :] = words[:, c * LANES:(c + 1) * LANES]


def _load_rows(ref, first_row, rows, nw):
    return jnp.concatenate(
        [ref[pl.ds(first_row * nw + c, rows, stride=nw), :] for c in range(nw)], axis=1)


def _router_kernel(h_ref, g_ref, whi_ref, wlo_ref, b_ref, tri_ref, hn_ref, ei_ref, ew_ref, cnt_ref, base_ref,
                   *, n_groups, epg):
    @pl.when(pl.program_id(0) == 0)
    def _():
        base_ref[...] = jnp.zeros_like(base_ref)

    hn = _rms(h_ref[...], g_ref[...])
    _store_rows(hn_ref, _pack_halves(hn))
    hn_hi = hn.astype(BF16)
    hn_lo = (hn - hn_hi.astype(F32)).astype(BF16)
    logits = (jnp.dot(hn_hi, whi_ref[...], preferred_element_type=F32)
              + jnp.dot(hn_lo, whi_ref[...], preferred_element_type=F32)
              + jnp.dot(hn_hi, wlo_ref[...], preferred_element_type=F32)) + b_ref[...]
    lane = lax.broadcasted_iota(jnp.int32, logits.shape, 1)
    n_lanes = logits.shape[1]

    def first_argmax(vals, vmax):
        first = jnp.min(jnp.where(vals == vmax, lane, n_lanes).astype(F32), axis=-1, keepdims=True)
        return first.astype(jnp.int32)

    in_groups = lane < n_groups
    gl = jnp.where(in_groups, logits, -jnp.inf)
    g_max = jnp.max(gl, axis=-1, keepdims=True)
    g_sel = first_argmax(gl, g_max)
    p_group = 1.0 / jnp.sum(jnp.where(in_groups, jnp.exp(logits - g_max), 0.0), axis=-1, keepdims=True)

    lo = n_groups + g_sel * epg
    el = jnp.where(jnp.logical_and(lane >= lo, lane < lo + epg), logits, -jnp.inf)
    t1 = jnp.max(el, axis=-1, keepdims=True)
    i1 = first_argmax(el, t1)
    el2 = jnp.where(lane == i1, -jnp.inf, el)
    t2 = jnp.max(el2, axis=-1, keepdims=True)
    i2 = first_argmax(el2, t2)
    e2x = jnp.exp(t2 - t1)
    w1 = p_group / (1.0 + e2x)
    w2 = p_group * e2x / (1.0 + e2x)
    e1 = i1 - n_groups
    e2 = i2 - n_groups

    onehot = jnp.logical_or(lane == e1, lane == e2)
    earlier = jnp.dot(tri_ref[...], jnp.where(onehot, 1.0, 0.0).astype(BF16), preferred_element_type=F32)
    total = earlier + base_ref[...]
    r1 = jnp.sum(jnp.where(lane == e1, total, 0.0), axis=-1, keepdims=True).astype(jnp.int32)
    r2 = jnp.sum(jnp.where(lane == e2, total, 0.0), axis=-1, keepdims=True).astype(jnp.int32)
    base_ref[...] = base_ref[...] + jnp.sum(jnp.where(onehot, 1.0, 0.0), axis=0, keepdims=True)
    cnt_ref[...] = base_ref[...].astype(jnp.int32)

    ei_ref[...] = jnp.where(lane == 0, e1, jnp.where(lane == 1, e2, jnp.where(lane == 2, r1, jnp.where(lane == 3, r2, 0))))
    ew_ref[...] = jnp.where(lane == 0, w1, jnp.where(lane == 1, w2, 0.0))


def _router(h2, gain, w_router, b_router, n_groups, epg, tm):
    n, d = h2.shape
    tm = min(tm, n)
    wc = w_router.shape[1]
    nw = d // (2 * LANES)
    w_hi = w_router.astype(BF16)
    w_lo = (w_router - w_hi.astype(F32)).astype(BF16)
    tri = jnp.tri(tm, k=-1, dtype=BF16)
    kern = functools.partial(_router_kernel, n_groups=n_groups, epg=epg)
    return pl.pallas_call(
        kern,
        grid=(n // tm,),
        in_specs=[
            pl.BlockSpec((tm, d), lambda i: (i, 0)),
            pl.BlockSpec((1, d), lambda i: (0, 0)),
            pl.BlockSpec((d, wc), lambda i: (0, 0)),
            pl.BlockSpec((d, wc), lambda i: (0, 0)),
            pl.BlockSpec((1, wc), lambda i: (0, 0)),
            pl.BlockSpec((tm, tm), lambda i: (0, 0)),
        ],
        out_specs=[
            pl.BlockSpec((tm * nw, LANES), lambda i: (i, 0)),
            pl.BlockSpec((tm, wc), lambda i: (i, 0)),
            pl.BlockSpec((tm, wc), lambda i: (i, 0)),
            pl.BlockSpec((1, wc), lambda i: (0, 0)),
        ],
        out_shape=[
            jax.ShapeDtypeStruct((n * nw, LANES), jnp.uint32),
            jax.ShapeDtypeStruct((n, wc), jnp.int32),
            jax.ShapeDtypeStruct((n, wc), F32),
            jax.ShapeDtypeStruct((1, wc), jnp.int32),
        ],
        scratch_shapes=[pltpu.VMEM((1, wc), F32)],
        compiler_params=_params("arbitrary"),
        name="moe_router",
    )(h2, gain.reshape(1, d), w_hi, w_lo, b_router, tri)


def _slot_kernel(ei_ref, first_ref, dest_ref):
    ei = ei_ref[...]
    lane = lax.broadcasted_iota(jnp.int32, ei.shape, 1)
    cols = jnp.zeros(ei.shape, F32)
    for k in range(TOP_K):
        first = jnp.sum(jnp.where(lane == ei[:, k:k + 1], first_ref[...], 0.0), axis=-1, keepdims=True)
        slot = first + ei[:, TOP_K + k:TOP_K + k + 1].astype(F32)
        cols = jnp.where(lane == k, slot, cols)
    dest_ref[...] = cols.T[:TOP_K, :].astype(jnp.int32)


def _slots(ei, first_row, tm):
    n, wc = ei.shape
    tm = min(tm, n)
    return pl.pallas_call(
        _slot_kernel,
        grid=(n // tm,),
        in_specs=[
            pl.BlockSpec((tm, wc), lambda i: (i, 0)),
            pl.BlockSpec((1, wc), lambda i: (0, 0)),
        ],
        out_specs=pl.BlockSpec((TOP_K, tm), lambda i: (0, i)),
        out_shape=jax.ShapeDtypeStruct((TOP_K, n), jnp.int32),
        compiler_params=_params("parallel"),
        name="moe_slots",
    )(ei, first_row)


def _dispatch_kernel(dest_ref, last_ref, nv_ref, hn_ref, x_ref, zero_ref, zsem, tsem, rsem,
                     *, n_experts, n_blocks, tb, nw):
    i = pl.program_id(0)
    tm = hn_ref.shape[0] // nw
    n_tokens = pl.num_programs(0) * tm

    def row(ref, r):
        return ref.at[pl.ds(pl.multiple_of(r * nw, nw), nw), :]

    def zero_copy(block, sem):
        line0 = pl.multiple_of(jnp.maximum(block, 0) * (tb * nw), tb * nw)
        return pltpu.make_async_copy(zero_ref, x_ref.at[pl.ds(line0, tb * nw), :], sem)

    def for_expert_tails(fn):
        def body(e, c):
            @pl.when(last_ref[e] >= 0)
            def _():
                fn(zero_copy(last_ref[e], zsem.at[e]))
            return c
        lax.fori_loop(0, n_experts, body, 0)

    def for_unused_blocks(fn):
        def body(b, c):
            @pl.when(b >= nv_ref[0])
            def _():
                fn(zero_copy(b, tsem))
            return c
        lax.fori_loop(0, n_blocks, body, 0)

    @pl.when(i == 0)
    def _():
        zero_ref[...] = jnp.zeros_like(zero_ref)
        for_expert_tails(lambda cp: cp.start())
        for_unused_blocks(lambda cp: cp.start())
        for_expert_tails(lambda cp: cp.wait())
        for_unused_blocks(lambda cp: cp.wait())

    def start_one(j, c):
        for k in range(TOP_K):
            dst = dest_ref[k * n_tokens + i * tm + j]
            pltpu.make_async_copy(row(hn_ref, j), row(x_ref, dst), rsem).start()
        return c

    def wait_one(j, c):
        pltpu.make_async_copy(row(hn_ref, 0), row(x_ref, 0), rsem).wait()
        return c

    lax.fori_loop(0, tm, start_one, 0, unroll=4)
    lax.fori_loop(0, TOP_K * tm, wait_one, 0, unroll=8)


def _dispatch(hn_rows, dest, last_block, n_valid, n_blocks, nw, tm):
    n = hn_rows.shape[0] // nw
    tm = min(tm, n)
    n_experts = last_block.shape[0]
    tb = DISPATCH_BLOCK
    kern = functools.partial(_dispatch_kernel, n_experts=n_experts, n_blocks=n_blocks, tb=tb, nw=nw)
    grid_spec = pltpu.PrefetchScalarGridSpec(
        num_scalar_prefetch=3,
        grid=(n // tm,),
        in_specs=[pl.BlockSpec((tm * nw, LANES), lambda i, dest, last, nv: (i, 0))],
        out_specs=pl.BlockSpec(memory_space=pl.ANY),
        scratch_shapes=[
            pltpu.VMEM((tb * nw, LANES), jnp.uint32),
            pltpu.SemaphoreType.DMA((n_experts,)),
            pltpu.SemaphoreType.DMA,
            pltpu.SemaphoreType.DMA,
        ],
    )
    return pl.pallas_call(
        kern,
        grid_spec=grid_spec,
        out_shape=jax.ShapeDtypeStruct((n_blocks * tb * nw, LANES), jnp.uint32),
        compiler_params=_params("arbitrary"),
        name="moe_dispatch",
    )(dest, last_block, n_valid, hn_rows)


def _expert_kernel(be_ref, nv_ref, seg_ref, nxt_ref, x_ref, wg_hbm, wu_hbm, wd_hbm, o_ref,
                   wg_buf, wu_buf, wd_buf, wgb_ref, wub_ref, wdb_ref, sem):
    i = pl.program_id(0)
    valid = i < nv_ref[0]
    new_expert = jnp.logical_or(i == 0, be_ref[i] != be_ref[jnp.maximum(i - 1, 0)])
    slot = seg_ref[i] % 2

    def weight_copies(e, s):
        return (pltpu.make_async_copy(wg_hbm.at[e], wg_buf.at[s], sem.at[s, 0]),
                pltpu.make_async_copy(wu_hbm.at[e], wu_buf.at[s], sem.at[s, 1]),
                pltpu.make_async_copy(wd_hbm.at[e], wd_buf.at[s], sem.at[s, 2]))

    @pl.when(i == 0)
    def _():
        for cp in weight_copies(be_ref[0], 0):
            cp.start()

    @pl.when(jnp.logical_and(valid, new_expert))
    def _():
        @pl.when(nxt_ref[i] >= 0)
        def _():
            for cp in weight_copies(nxt_ref[i], 1 - slot):
                cp.start()

        for cp in weight_copies(be_ref[i], slot):
            cp.wait()
        wgb_ref[...] = wg_buf[slot].astype(BF16)
        wub_ref[...] = wu_buf[slot].astype(BF16)
        wdb_ref[...] = wd_buf[slot].astype(BF16)

    @pl.when(valid)
    def _():
        half = wgb_ref.shape[0] // 2
        nw = half // LANES
        x_lo, x_hi = _unpack_halves(_load_rows(x_ref, 0, x_ref.shape[0] // nw, nw))
        x_lo, x_hi = x_lo.astype(BF16), x_hi.astype(BF16)

        def proj(w_ref):
            return (jnp.dot(x_lo, w_ref[:half, :], preferred_element_type=F32)
                    + jnp.dot(x_hi, w_ref[half:, :], preferred_element_type=F32))

        hid = jax.nn.silu(proj(wgb_ref)) * proj(wub_ref)
        _store_rows(o_ref, _pack_halves(jnp.dot(hid.astype(BF16), wdb_ref[...], preferred_element_type=F32)))

    @pl.when(jnp.logical_not(valid))
    def _():
        o_ref[...] = jnp.zeros_like(o_ref)


def _experts(x_disp, block_expert, n_valid, block_segment, next_expert, w_gate, w_up, w_down):
    d, de = w_gate.shape[1], w_gate.shape[2]
    nw = d // (2 * LANES)
    rows = x_disp.shape[0] // nw
    tb = DISPATCH_BLOCK
    grid_spec = pltpu.PrefetchScalarGridSpec(
        num_scalar_prefetch=4,
        grid=(rows // tb,),
        in_specs=[
            pl.BlockSpec((tb * nw, LANES),
                         lambda i, be, nv, seg, nxt: (jnp.maximum(jnp.minimum(i, nv[0] - 1), 0), 0)),
            pl.BlockSpec(memory_space=pl.ANY),
            pl.BlockSpec(memory_space=pl.ANY),
            pl.BlockSpec(memory_space=pl.ANY),
        ],
        out_specs=pl.BlockSpec((tb * nw, LANES), lambda i, be, nv, seg, nxt: (i, 0)),
        scratch_shapes=[
            pltpu.VMEM((2, d, de), w_gate.dtype),
            pltpu.VMEM((2, d, de), w_up.dtype),
            pltpu.VMEM((2, de, d), w_down.dtype),
            pltpu.VMEM((d, de), BF16),
            pltpu.VMEM((d, de), BF16),
            pltpu.VMEM((de, d), BF16),
            pltpu.SemaphoreType.DMA((2, 3)),
        ],
    )
    return pl.pallas_call(
        _expert_kernel,
        grid_spec=grid_spec,
        out_shape=jax.ShapeDtypeStruct((rows * nw, LANES), jnp.uint32),
        compiler_params=_params("arbitrary"),
        name="moe_experts",
    )(block_expert, n_valid, block_segment, next_expert, x_disp, w_gate, w_up, w_down)


def _combine_kernel(dest_ref, h_ref, ew_ref, g_ref, y_ref, o_ref, buf_ref, sem, *, final_norm):
    tm, d = h_ref.shape
    nw = d // (2 * LANES)
    i = pl.program_id(0)
    n_tiles = pl.num_programs(0)

    def row(ref, r):
        return ref.at[pl.ds(pl.multiple_of(r * nw, nw), nw), :]

    def start_tile(tile, slot):
        def start_one(j, c):
            for k in range(TOP_K):
                src = dest_ref[k * (n_tiles * tm) + tile * tm + j]
                pltpu.make_async_copy(row(y_ref, src), row(buf_ref.at[slot], k * tm + j), sem.at[slot]).start()
            return c

        lax.fori_loop(0, tm, start_one, 0, unroll=4)

    def wait_tile(slot):
        def wait_one(j, c):
            pltpu.make_async_copy(row(y_ref, 0), row(buf_ref.at[slot], 0), sem.at[slot]).wait()
            return c

        lax.fori_loop(0, TOP_K * tm, wait_one, 0, unroll=8)

    slot = i % 2

    @pl.when(i == 0)
    def _():
        start_tile(0, 0)

    @pl.when(i + 1 < n_tiles)
    def _():
        start_tile(i + 1, 1 - slot)

    wait_tile(slot)
    h = h_ref[...]
    w = ew_ref[...]
    lo, hi = h[:, :d // 2], h[:, d // 2:]
    for k in range(TOP_K):
        y_lo, y_hi = _unpack_halves(_load_rows(buf_ref.at[slot], k * tm, tm, nw))
        lo = lo + y_lo * w[:, k:k + 1]
        hi = hi + y_hi * w[:, k:k + 1]
    if final_norm:
        ms = (jnp.sum(lo * lo, axis=-1, keepdims=True) + jnp.sum(hi * hi, axis=-1, keepdims=True)) / d
        scale = lax.rsqrt(ms + RMS_EPS)
        lo = lo * scale * g_ref[:, :d // 2]
        hi = hi * scale * g_ref[:, d // 2:]
    o_ref[:, :d // 2] = lo
    o_ref[:, d // 2:] = hi


def _combine(h2, y_disp, dest, ew, gain, final_norm, tm):
    n, d = h2.shape
    tm = min(tm, n)
    nw = d // (2 * LANES)
    wc = ew.shape[1]
    kern = functools.partial(_combine_kernel, final_norm=final_norm)
    grid_spec = pltpu.PrefetchScalarGridSpec(
        num_scalar_prefetch=1,
        grid=(n // tm,),
        in_specs=[
            pl.BlockSpec((tm, d), lambda i, dest: (i, 0)),
            pl.BlockSpec((tm, wc), lambda i, dest: (i, 0)),
            pl.BlockSpec((1, d), lambda i, dest: (0, 0)),
            pl.BlockSpec(memory_space=pl.ANY),
        ],
        out_specs=pl.BlockSpec((tm, d), lambda i, dest: (i, 0)),
        scratch_shapes=[
            pltpu.VMEM((2, TOP_K * tm * nw, LANES), jnp.uint32),
            pltpu.SemaphoreType.DMA((2,)),
        ],
    )
    return pl.pallas_call(
        kern,
        grid_spec=grid_spec,
        out_shape=jax.ShapeDtypeStruct((n, d), F32),
        compiler_params=_params("arbitrary"),
        name="moe_combine",
    )(dest, h2, ew, gain.reshape(1, d), y_disp)


def _hier_moe(h2, norm_gain, w_group, b_group, w_expert, b_expert, w_gate, w_up, w_down, final_gain):
    n, d = h2.shape
    n_groups = w_group.shape[1]
    n_experts = w_expert.shape[1]
    epg = n_experts // n_groups
    n_logits = n_groups + n_experts
    wc = -(-n_logits // LANES) * LANES
    w_router = jnp.concatenate(
        [w_group.astype(F32), w_expert.astype(F32), jnp.zeros((d, wc - n_logits), F32)], axis=1)
    b_router = jnp.concatenate(
        [b_group.astype(F32), b_expert.astype(F32), jnp.zeros((wc - n_logits,), F32)]).reshape(1, wc)
    hn_packed, ei, ew, counts = _router(h2, norm_gain, w_router, b_router, n_groups, epg, tm=512)

    tb = DISPATCH_BLOCK
    n_pairs = n * TOP_K
    counts = counts[0, :n_experts]
    blocks_per_expert = (counts + tb - 1) // tb
    block_end = jnp.cumsum(blocks_per_expert)
    block_start = block_end - blocks_per_expert
    first_row = jnp.pad((block_start * tb).astype(F32), (0, wc - n_experts)).reshape(1, wc)
    dest = _slots(ei, first_row, tm=512).reshape(-1)
    last_block = jnp.where(blocks_per_expert > 0, block_end - 1, -1).astype(jnp.int32)
    n_blocks = -(-n_pairs // tb) + n_experts
    n_valid = block_end[-1:].astype(jnp.int32)
    blk = jnp.minimum(jnp.arange(n_blocks, dtype=jnp.int32), n_valid[0] - 1)
    block_expert = jnp.searchsorted(block_end, blk, side="right").astype(jnp.int32)
    expert_range = jnp.arange(n_experts, dtype=jnp.int32)
    owners = jnp.where(blocks_per_expert > 0, expert_range, n_experts)
    owner_at_or_after = lax.cummin(owners, axis=0, reverse=True)
    owner_after = jnp.concatenate([owner_at_or_after[1:], jnp.full((1,), n_experts, jnp.int32)])
    next_expert = jnp.take(jnp.where(owner_after < n_experts, owner_after, -1), block_expert).astype(jnp.int32)
    changed = jnp.concatenate([jnp.zeros((1,), jnp.int32),
                               (block_expert[1:] != block_expert[:-1]).astype(jnp.int32)])
    block_segment = jnp.cumsum(changed).astype(jnp.int32)

    x_disp = _dispatch(hn_packed, dest, last_block, n_valid, n_blocks, d // (2 * LANES), tm=512)
    y_disp = _experts(x_disp, block_expert, n_valid, block_segment, next_expert, w_gate, w_up, w_down)
    gain = norm_gain if final_gain is None else final_gain
    return _combine(h2, y_disp, dest, ew, gain, final_gain is not None, tm=512)


def _forget_kernel(f_ref, b_ref, c_ref):
    heads = c_ref.shape[0]
    x = jax.nn.log_sigmoid(f_ref[...].T[:heads, :] + b_ref[...])
    s = x.shape[1]
    lane = lax.broadcasted_iota(jnp.int32, x.shape, 1)
    shift = 1
    while shift < s:
        x = x + jnp.where(lane >= shift, pltpu.roll(x, shift, axis=1), 0.0)
        shift *= 2
    c_ref[...] = x


def _forget_cumsum(f_logit, b_forget, b, s):
    h = b_forget.shape[0]
    return pl.pallas_call(
        _forget_kernel,
        grid=(b,),
        in_specs=[
            pl.BlockSpec((s, f_logit.shape[1]), lambda i: (i, 0)),
            pl.BlockSpec((h, 1), lambda i: (0, 0)),
        ],
        out_specs=pl.BlockSpec((None, h, s), lambda i: (i, 0, 0)),
        out_shape=jax.ShapeDtypeStruct((b, h, s), F32),
        compiler_params=_params("parallel"),
        name="fox_forget",
    )(f_logit, b_forget.astype(F32).reshape(h, 1))


def _fox_attn_kernel(q_ref, k_ref, v_ref, c_ref, o_ref, *, tk, n_sub):
    seq, dh = q_ref.shape
    head = pl.program_id(1)
    log2e = math.log2(math.e)
    row = lax.broadcasted_iota(jnp.int32, (tk, tk), 0)
    col = lax.broadcasted_iota(jnp.int32, (tk, tk), 1)
    causal = col <= row

    def attend(q, k, v, bias, carry, mask):
        m, l, acc = carry
        s = lax.dot_general(q, k, (((1,), (1,)), ((), ())), preferred_element_type=F32) + bias
        if mask:
            s = jnp.where(causal, s, -jnp.inf)
        m_new = jnp.maximum(m, jnp.max(s, axis=-1, keepdims=True))
        alpha = jnp.exp2(m - m_new)
        p = jnp.exp2(s - m_new)
        l = alpha * l + jnp.sum(p, axis=-1, keepdims=True)
        acc = alpha * acc + jnp.dot(p.astype(BF16), v, preferred_element_type=F32)
        return m_new, l, acc

    def key_block(k0):
        bias = c_ref[pl.ds(head, 1), pl.ds(k0, tk)] * (-log2e)
        return k_ref[pl.ds(k0, tk), :], v_ref[pl.ds(k0, tk), :], bias

    for qt in range(seq // (tk * n_sub)):
        first = qt * n_sub
        qs = [q_ref[(first + a) * tk:(first + a + 1) * tk, :] for a in range(n_sub)]
        init = (jnp.full((tk, 1), -jnp.inf, F32), jnp.zeros((tk, 1), F32), jnp.zeros((tk, dh), F32))
        carries = (init,) * n_sub

        def kv_step(ki, carries, qs=qs):
            k, v, bias = key_block(pl.multiple_of(ki * tk, tk))
            return tuple(attend(qs[a], k, v, bias, carries[a], False) for a in range(n_sub))

        if qt > 0:
            carries = lax.fori_loop(0, first, kv_step, carries)
        carries = list(carries)
        for kb in range(n_sub):
            k, v, bias = key_block((first + kb) * tk)
            for a in range(kb, n_sub):
                carries[a] = attend(qs[a], k, v, bias, carries[a], a == kb)
        for a in range(n_sub):
            _, l, acc = carries[a]
            o_ref[(first + a) * tk:(first + a + 1) * tk, :] = (acc / l).astype(o_ref.dtype)


def _fox_attention(proj, cum, batch, seq, heads, tk, n_sub):
    dh = FOX_HEAD_DIM
    tk = min(tk, seq)
    n_sub = min(n_sub, seq // tk)
    kern = functools.partial(_fox_attn_kernel, tk=tk, n_sub=n_sub)
    return pl.pallas_call(
        kern,
        grid=(batch, heads),
        in_specs=[
            pl.BlockSpec((seq, dh), lambda b, h: (b, h)),
            pl.BlockSpec((seq, dh), lambda b, h: (b, heads + h)),
            pl.BlockSpec((seq, dh), lambda b, h: (b, 2 * heads + h)),
            pl.BlockSpec((None, heads, seq), lambda b, h: (b, 0, 0)),
        ],
        out_specs=pl.BlockSpec((seq, dh), lambda b, h: (b, h)),
        out_shape=jax.ShapeDtypeStruct((batch * seq, heads * dh), BF16),
        compiler_params=_params("parallel", "parallel"),
        name="fox_attention",
    )(proj, proj, proj, cum)


def _fox_out_kernel(o_ref, gate_ref, w_ref, h_ref, out_ref):
    gated = (o_ref[...].astype(F32) * jax.nn.sigmoid(gate_ref[...].astype(F32))).astype(BF16)
    out_ref[...] = h_ref[...] + jnp.dot(gated, w_ref[...], preferred_element_type=F32)


def _fox_out(o, proj, w_out, h2, tm):
    n, d = h2.shape
    tm = min(tm, n)
    gate_block = 3
    return pl.pallas_call(
        _fox_out_kernel,
        grid=(n // tm,),
        in_specs=[
            pl.BlockSpec((tm, d), lambda i: (i, 0)),
            pl.BlockSpec((tm, d), lambda i: (i, gate_block)),
            pl.BlockSpec((d, d), lambda i: (0, 0)),
            pl.BlockSpec((tm, d), lambda i: (i, 0)),
        ],
        out_specs=pl.BlockSpec((tm, d), lambda i: (i, 0)),
        out_shape=jax.ShapeDtypeStruct((n, d), F32),
        compiler_params=_params("parallel"),
        name="fox_out",
    )(o, proj, w_out, h2)


def _s5_layer(x, norm_gain, w_in, lam_re, lam_im, b_re, b_im, c_re, c_im, d_skip, log_step, w_glu, b_glu, w_out):
    b, s, d = x.shape
    assert b == STATE_BATCH
    ds = w_in.shape[1]
    bmat, cmat, a_re, a_im = _s5_discretise(lam_re, lam_im, b_re, b_im, c_re, c_im, log_step)
    u_tm = _rms_matmul(x, norm_gain, w_in.astype(BF16), jnp.ones((ds,), F32), F32, True, ts=1024, tn=ds)
    y_tm = _s5_scan(u_tm.reshape(s * b, ds), bmat, cmat, a_re, a_im, d_skip.astype(F32), t_chunk=128)
    return _s5_out(y_tm.reshape(s, b * ds), x, w_glu.astype(BF16), b_glu.astype(F32), w_out.astype(BF16), ts=512)


def _fox_layer(h, norm_gain, w_in, b_forget, w_out):
    b, s, d = h.shape
    heads = d // FOX_HEAD_DIM
    w_main = w_in[:, :4 * d].astype(BF16)
    w_forget = jnp.pad(w_in[:, 4 * d:], ((0, 0), (0, LANES - heads))).astype(BF16)
    q_scale = FOX_HEAD_DIM ** -0.5 * math.log2(math.e)
    col_scale = jnp.concatenate([jnp.full((d,), q_scale, F32), jnp.ones((3 * d,), F32)])
    proj, f_logit = _rms_matmul(h, norm_gain, w_main, col_scale, BF16, False, ts=1024, tn=1024, w_side=w_forget)
    cum = _forget_cumsum(f_logit, b_forget, b, s)
    o = _fox_attention(proj, cum, b, s, heads, tk=512, n_sub=4)
    return _fox_out(o, proj, w_out.astype(BF16), h.reshape(b * s, d), tm=512).reshape(b, s, d)


def kernel(x, l0_mix_norm, l0_s5_w_in, l0_s5_lambda_re, l0_s5_lambda_im, l0_s5_b_re, l0_s5_b_im, l0_s5_c_re, l0_s5_c_im, l0_s5_d, l0_s5_log_step, l0_s5_w_glu, l0_s5_b_glu, l0_s5_w_out, l0_ffn_norm, l0_moe_w_group, l0_moe_b_group, l0_moe_w_expert, l0_moe_b_expert, l0_moe_w_gate, l0_moe_w_up, l0_moe_w_down, l1_mix_norm, l1_fox_w_in, l1_fox_b_forget, l1_fox_w_out, l1_ffn_norm, l1_moe_w_group, l1_moe_b_group, l1_moe_w_expert, l1_moe_b_expert, l1_moe_w_gate, l1_moe_w_up, l1_moe_w_down, final_norm):
    b, s, d = x.shape
    h = _s5_layer(x, l0_mix_norm, l0_s5_w_in, l0_s5_lambda_re, l0_s5_lambda_im, l0_s5_b_re, l0_s5_b_im,
                  l0_s5_c_re, l0_s5_c_im, l0_s5_d, l0_s5_log_step, l0_s5_w_glu, l0_s5_b_glu, l0_s5_w_out)
    h = _hier_moe(h.reshape(b * s, d), l0_ffn_norm, l0_moe_w_group, l0_moe_b_group, l0_moe_w_expert,
                  l0_moe_b_expert, l0_moe_w_gate, l0_moe_w_up, l0_moe_w_down, None).reshape(b, s, d)
    h = _fox_layer(h, l1_mix_norm, l1_fox_w_in, l1_fox_b_forget, l1_fox_w_out)
    h = _hier_moe(h.reshape(b * s, d), l1_ffn_norm, l1_moe_w_group, l1_moe_b_group, l1_moe_w_expert,
                  l1_moe_b_expert, l1_moe_w_gate, l1_moe_w_up, l1_moe_w_down, final_norm)
    return h.reshape(b, s, d)
```

```python
import functools
import math

import jax
import jax.numpy as jnp
from jax import lax
from jax.experimental import pallas as pl
from jax.experimental.pallas import tpu as pltpu

F32 = jnp.float32
BF16 = jnp.bfloat16

RMS_EPS = 1e-6
STATE_BATCH = 8
SSM_GROUPS_PER_BLOCK = 8
FOX_HEAD_DIM = 128
TOP_K = 2
DISPATCH_BLOCK = 256
LANES = 128
VMEM_LIMIT = 56 * 1024 * 1024


def _params(*sem):
    return pltpu.CompilerParams(dimension_semantics=sem, vmem_limit_bytes=VMEM_LIMIT)


def _rms(x, gain):
    ms = jnp.mean(x * x, axis=-1, keepdims=True)
    return x * lax.rsqrt(ms + RMS_EPS) * gain


def _rms_matmul_kernel(*refs, has_side):
    if has_side:
        x_ref, g_ref, w_ref, cs_ref, ws_ref, o_ref, side_ref, xn_ref = refs
    else:
        x_ref, g_ref, w_ref, cs_ref, o_ref, xn_ref = refs

    @pl.when(pl.program_id(2) == 0)
    def _():
        xn_ref[...] = _rms(x_ref[...], g_ref[...]).astype(BF16)
        if has_side:
            side_ref[...] = jnp.dot(xn_ref[...], ws_ref[...], preferred_element_type=F32)

    acc = jnp.dot(xn_ref[...], w_ref[...], preferred_element_type=F32)
    o_ref[...] = (acc * cs_ref[...]).astype(o_ref.dtype)


def _rms_matmul(x, gain, w, n_cols, col_scale, out_dtype, ts, tn, w_side=None):
    b, s, d = x.shape
    m = n_cols
    ts, tn = min(ts, s), min(tn, m)
    nj = m // tn
    ns = s // ts
    out_shape = [jax.ShapeDtypeStruct((b * s, m), out_dtype)]
    out_specs = [pl.BlockSpec((ts, tn), lambda bi, si, j: (bi * ns + si, j))]
    in_specs = [
        pl.BlockSpec((None, ts, d), lambda bi, si, j: (bi, si, 0)),
        pl.BlockSpec((1, d), lambda bi, si, j: (0, 0)),
        pl.BlockSpec((d, tn), lambda bi, si, j: (0, j)),
        pl.BlockSpec((1, tn), lambda bi, si, j: (0, j)),
    ]
    args = [x, gain.reshape(1, d), w, col_scale.reshape(1, m)]
    if w_side is not None:
        ms = w_side.shape[1]
        in_specs.append(pl.BlockSpec((d, ms), lambda bi, si, j: (0, 0)))
        args.append(w_side)
        out_shape.append(jax.ShapeDtypeStruct((b * s, ms), F32))
        out_specs.append(pl.BlockSpec((ts, ms), lambda bi, si, j: (bi * ns + si, 0)))
    outs = pl.pallas_call(
        functools.partial(_rms_matmul_kernel, has_side=w_side is not None),
        grid=(b, ns, nj),
        in_specs=in_specs,
        out_specs=out_specs,
        out_shape=out_shape,
        scratch_shapes=[pltpu.VMEM((ts, d), BF16)],
        compiler_params=_params("parallel", "parallel", "arbitrary"),
        name="rms_matmul",
    )(*args)
    return outs[0] if w_side is None else outs


def _s5_in_kernel(x_ref, g_ref, w_ref, u_ref):
    nb, ts, d = x_ref.shape
    nl = w_ref.shape[1] // LANES
    xn = _rms(x_ref[...].reshape(nb * ts, d), g_ref[...]).astype(BF16)
    u = jnp.dot(xn, w_ref[...], preferred_element_type=F32)
    for bi in range(nb):
        for c in range(nl):
            u_ref[pl.ds(bi * nl + c, ts, stride=nb * nl), :] = u[bi * ts:(bi + 1) * ts, c * LANES:(c + 1) * LANES]


def _s5_in(x, gain, w, ts):
    b, s, d = x.shape
    m = w.shape[1]
    nl = m // LANES
    ts = min(ts, s)
    return pl.pallas_call(
        _s5_in_kernel,
        grid=(s // ts,),
        in_specs=[
            pl.BlockSpec((b, ts, d), lambda si: (0, si, 0)),
            pl.BlockSpec((1, d), lambda si: (0, 0)),
            pl.BlockSpec((d, m), lambda si: (0, 0)),
        ],
        out_specs=pl.BlockSpec((ts * b * nl, LANES), lambda si: (si, 0)),
        out_shape=jax.ShapeDtypeStruct((s * b * nl, LANES), F32),
        compiler_params=_params("parallel"),
        name="s5_in",
    )(x, gain.reshape(1, d), w)


def _s5_scan_kernel(u_ref, bmat_ref, cmat_ref, are_ref, aim_ref, d_ref, y_ref, bu2_ref, st_ref, *, n_blocks, t_chunk):
    @pl.when(pl.program_id(0) == 0)
    def _():
        st_ref[...] = jnp.zeros_like(st_ref)

    cw = bmat_ref.shape[1]
    sw = bmat_ref.shape[2] // 2
    rows = t_chunk * STATE_BATCH
    for gb in range(n_blocks):
        u_blk = u_ref[pl.ds(gb, rows, stride=n_blocks), :]
        bu_ref = bu2_ref.at[gb % 2]
        bu_ref[...] = jnp.dot(u_blk.astype(BF16), bmat_ref[gb], preferred_element_type=F32)
        a_re = jnp.broadcast_to(are_ref[gb], (STATE_BATCH, sw))
        a_im = jnp.broadcast_to(aim_ref[gb], (STATE_BATCH, sw))

        def step(t, carry):
            s_re, s_im = carry
            r0 = pl.multiple_of(t * STATE_BATCH, STATE_BATCH)
            b_re = bu_ref[pl.ds(r0, STATE_BATCH), 0:sw]
            b_im = bu_ref[pl.ds(r0, STATE_BATCH), sw:2 * sw]
            n_re = a_re * s_re - a_im * s_im + b_re
            n_im = a_re * s_im + a_im * s_re + b_im
            bu_ref[pl.ds(r0, STATE_BATCH), 0:sw] = n_re
            bu_ref[pl.ds(r0, STATE_BATCH), sw:2 * sw] = n_im
            return n_re, n_im

        s_re, s_im = lax.fori_loop(0, t_chunk, step, (st_ref[gb, 0], st_ref[gb, 1]), unroll=True)
        st_ref[gb, 0] = s_re
        st_ref[gb, 1] = s_im
        y = jnp.dot(bu_ref[...].astype(BF16), cmat_ref[gb], preferred_element_type=F32)
        y_ref[pl.ds(gb, rows, stride=n_blocks), :] = y + d_ref[:, gb * cw:(gb + 1) * cw] * u_blk


def _s5_scan(u_tm, bmat, cmat, a_re, a_im, d_skip, t_chunk):
    nb, cw, sw2 = bmat.shape
    assert cw == LANES
    ds = nb * cw
    seq = u_tm.shape[0] // (STATE_BATCH * nb)
    t_chunk = min(t_chunk, seq)
    tr = t_chunk * STATE_BATCH
    kern = functools.partial(_s5_scan_kernel, n_blocks=nb, t_chunk=t_chunk)
    return pl.pallas_call(
        kern,
        grid=(seq // t_chunk,),
        in_specs=[
            pl.BlockSpec((tr * nb, LANES), lambda i: (i, 0)),
            pl.BlockSpec((nb, cw, sw2), lambda i: (0, 0, 0)),
            pl.BlockSpec((nb, sw2, cw), lambda i: (0, 0, 0)),
            pl.BlockSpec((nb, 1, sw2 // 2), lambda i: (0, 0, 0)),
            pl.BlockSpec((nb, 1, sw2 // 2), lambda i: (0, 0, 0)),
            pl.BlockSpec((1, ds), lambda i: (0, 0)),
        ],
        out_specs=pl.BlockSpec((tr * nb, LANES), lambda i: (i, 0)),
        out_shape=jax.ShapeDtypeStruct(u_tm.shape, F32),
        scratch_shapes=[
            pltpu.VMEM((2, tr, sw2), F32),
            pltpu.VMEM((nb, 2, STATE_BATCH, sw2 // 2), F32),
        ],
        compiler_params=_params("arbitrary"),
        name="s5_scan",
    )(u_tm, bmat, cmat, a_re, a_im, d_skip.reshape(1, ds))


def _s5_discretise(lam_re, lam_im, b_re, b_im, c_re, c_im, log_step):
    lr, li = lam_re.astype(F32), lam_im.astype(F32)
    step = jnp.exp(log_step.astype(F32))[:, None]
    mag = jnp.exp(lr * step)
    ab_re = mag * jnp.cos(li * step)
    ab_im = mag * jnp.sin(li * step)
    den = lr * lr + li * li
    nr, ni = ab_re - 1.0, ab_im
    fr = (nr * lr + ni * li) / den
    fi = (ni * lr - nr * li) / den
    br, bi = b_re.astype(F32), b_im.astype(F32)
    bb_re = fr[..., None] * br - fi[..., None] * bi
    bb_im = fr[..., None] * bi + fi[..., None] * br
    g, n, c = bb_re.shape
    gpb = min(SSM_GROUPS_PER_BLOCK, g)
    nb = g // gpb
    eye = jnp.eye(gpb, dtype=F32)

    def in_blocks(bb):
        return jnp.einsum("bjnc,jk->bjckn", bb.reshape(nb, gpb, n, c), eye).reshape(nb, gpb * c, gpb * n)

    def out_blocks(cc):
        return jnp.einsum("bjcn,jk->bjnkc", cc.reshape(nb, gpb, c, n), eye).reshape(nb, gpb * n, gpb * c)

    bmat = jnp.concatenate([in_blocks(bb_re), in_blocks(bb_im)], axis=-1).astype(BF16)
    cmat = jnp.concatenate([out_blocks(c_re.astype(F32)), -out_blocks(c_im.astype(F32))], axis=1).astype(BF16)
    return bmat, cmat, ab_re.reshape(nb, 1, gpb * n), ab_im.reshape(nb, 1, gpb * n)


def _s5_out_kernel(y_ref, x_ref, wg_ref, bg_ref, wo_ref, o_ref):
    nb, ts, d = x_ref.shape
    nl = wg_ref.shape[0] // LANES
    y = jnp.concatenate(
        [jnp.concatenate([y_ref[pl.ds(bi * nl + c, ts, stride=nb * nl), :] for c in range(nl)], axis=1)
         for bi in range(nb)], axis=0)
    g = jax.nn.gelu(y)
    z = jnp.dot(g.astype(BF16), wg_ref[...], preferred_element_type=F32) + bg_ref[...]
    gated = g * jax.nn.sigmoid(z)
    out = jnp.dot(gated.astype(BF16), wo_ref[...], preferred_element_type=F32)
    o_ref[...] = x_ref[...] + out.reshape(nb, ts, d)


def _s5_out(y_tm, x, w_glu, b_glu, w_out, ts):
    b, s, d = x.shape
    ds = w_glu.shape[0]
    nl = ds // LANES
    ts = min(ts, s)
    return pl.pallas_call(
        _s5_out_kernel,
        grid=(s // ts,),
        in_specs=[
            pl.BlockSpec((ts * b * nl, LANES), lambda si: (si, 0)),
            pl.BlockSpec((b, ts, d), lambda si: (0, si, 0)),
            pl.BlockSpec((ds, ds), lambda si: (0, 0)),
            pl.BlockSpec((1, ds), lambda si: (0, 0)),
            pl.BlockSpec((ds, d), lambda si: (0, 0)),
        ],
        out_specs=pl.BlockSpec((b, ts, d), lambda si: (0, si, 0)),
        out_shape=jax.ShapeDtypeStruct((b, s, d), F32),
        compiler_params=_params("parallel"),
        name="s5_out",
    )(y_tm, x, w_glu, b_glu.reshape(1, ds), w_out)


def _pack_halves(x):
    half = x.shape[-1] // 2
    lo = lax.bitcast_convert_type(x[:, :half].astype(BF16).astype(F32), jnp.uint32)
    hi = lax.bitcast_convert_type(x[:, half:].astype(BF16).astype(F32), jnp.uint32)
    return (lo >> 16) | (hi & jnp.uint32(0xFFFF0000))


def _unpack_halves(words):
    lo = lax.bitcast_convert_type(words << 16, F32)
    hi = lax.bitcast_convert_type(words & jnp.uint32(0xFFFF0000), F32)
    return lo, hi


def _store_rows(ref, words):
    rows, nw = words.shape[0], words.shape[1] // LANES
    for c in range(nw):
        ref[pl.ds(c, rows, stride=nw), :] = words[:, c * LANES:(c + 1) * LANES]


def _load_rows(ref, first_row, rows, nw):
    return jnp.concatenate(
        [ref[pl.ds(first_row * nw + c, rows, stride=nw), :] for c in range(nw)], axis=1)


def _router_kernel(h_ref, g_ref, whi_ref, wlo_ref, b_ref, tri_ref, hn_ref, ei_ref, ew_ref, cnt_ref, base_ref,
                   *, n_groups, epg):
    @pl.when(pl.program_id(0) == 0)
    def _():
        base_ref[...] = jnp.zeros_like(base_ref)

    hn = _rms(h_ref[...], g_ref[...])
    _store_rows(hn_ref, _pack_halves(hn))
    hn_hi = hn.astype(BF16)
    hn_lo = (hn - hn_hi.astype(F32)).astype(BF16)
    logits = (jnp.dot(hn_hi, whi_ref[...], preferred_element_type=F32)
              + jnp.dot(hn_lo, whi_ref[...], preferred_element_type=F32)
              + jnp.dot(hn_hi, wlo_ref[...], preferred_element_type=F32)) + b_ref[...]
    lane = lax.broadcasted_iota(jnp.int32, logits.shape, 1)
    n_lanes = logits.shape[1]

    def first_argmax(vals, vmax):
        first = jnp.min(jnp.where(vals == vmax, lane, n_lanes).astype(F32), axis=-1, keepdims=True)
        return first.astype(jnp.int32)

    in_groups = lane < n_groups
    gl = jnp.where(in_groups, logits, -jnp.inf)
    g_max = jnp.max(gl, axis=-1, keepdims=True)
    g_sel = first_argmax(gl, g_max)
    p_group = 1.0 / jnp.sum(jnp.where(in_groups, jnp.exp(logits - g_max), 0.0), axis=-1, keepdims=True)

    lo = n_groups + g_sel * epg
    el = jnp.where(jnp.logical_and(lane >= lo, lane < lo + epg), logits, -jnp.inf)
    t1 = jnp.max(el, axis=-1, keepdims=True)
    i1 = first_argmax(el, t1)
    el2 = jnp.where(lane == i1, -jnp.inf, el)
    t2 = jnp.max(el2, axis=-1, keepdims=True)
    i2 = first_argmax(el2, t2)
    e2x = jnp.exp(t2 - t1)
    w1 = p_group / (1.0 + e2x)
    w2 = p_group * e2x / (1.0 + e2x)
    e1 = i1 - n_groups
    e2 = i2 - n_groups

    onehot = jnp.logical_or(lane == e1, lane == e2)
    earlier = jnp.dot(tri_ref[...], jnp.where(onehot, 1.0, 0.0).astype(BF16), preferred_element_type=F32)
    total = earlier + base_ref[...]
    r1 = jnp.sum(jnp.where(lane == e1, total, 0.0), axis=-1, keepdims=True).astype(jnp.int32)
    r2 = jnp.sum(jnp.where(lane == e2, total, 0.0), axis=-1, keepdims=True).astype(jnp.int32)
    base_ref[...] = base_ref[...] + jnp.sum(jnp.where(onehot, 1.0, 0.0), axis=0, keepdims=True)
    cnt_ref[...] = base_ref[...].astype(jnp.int32)

    ei_ref[...] = jnp.where(lane == 0, e1, jnp.where(lane == 1, e2, jnp.where(lane == 2, r1, jnp.where(lane == 3, r2, 0))))
    ew_ref[...] = jnp.where(lane == 0, w1, jnp.where(lane == 1, w2, 0.0))


def _router(h2, gain, w_router, b_router, n_groups, epg, tm):
    n, d = h2.shape
    tm = min(tm, n)
    wc = w_router.shape[1]
    nw = d // (2 * LANES)
    w_hi = w_router.astype(BF16)
    w_lo = (w_router - w_hi.astype(F32)).astype(BF16)
    tri = jnp.tri(tm, k=-1, dtype=BF16)
    kern = functools.partial(_router_kernel, n_groups=n_groups, epg=epg)
    return pl.pallas_call(
        kern,
        grid=(n // tm,),
        in_specs=[
            pl.BlockSpec((tm, d), lambda i: (i, 0)),
            pl.BlockSpec((1, d), lambda i: (0, 0)),
            pl.BlockSpec((d, wc), lambda i: (0, 0)),
            pl.BlockSpec((d, wc), lambda i: (0, 0)),
            pl.BlockSpec((1, wc), lambda i: (0, 0)),
            pl.BlockSpec((tm, tm), lambda i: (0, 0)),
        ],
        out_specs=[
            pl.BlockSpec((tm * nw, LANES), lambda i: (i, 0)),
            pl.BlockSpec((tm, wc), lambda i: (i, 0)),
            pl.BlockSpec((tm, wc), lambda i: (i, 0)),
            pl.BlockSpec((1, wc), lambda i: (0, 0)),
        ],
        out_shape=[
            jax.ShapeDtypeStruct((n * nw, LANES), jnp.uint32),
            jax.ShapeDtypeStruct((n, wc), jnp.int32),
            jax.ShapeDtypeStruct((n, wc), F32),
            jax.ShapeDtypeStruct((1, wc), jnp.int32),
        ],
        scratch_shapes=[pltpu.VMEM((1, wc), F32)],
        compiler_params=_params("arbitrary"),
        name="moe_router",
    )(h2, gain.reshape(1, d), w_hi, w_lo, b_router, tri)


def _slot_kernel(ei_ref, first_ref, dest_ref):
    ei = ei_ref[...]
    lane = lax.broadcasted_iota(jnp.int32, ei.shape, 1)
    cols = jnp.zeros(ei.shape, F32)
    for k in range(TOP_K):
        first = jnp.sum(jnp.where(lane == ei[:, k:k + 1], first_ref[...], 0.0), axis=-1, keepdims=True)
        slot = first + ei[:, TOP_K + k:TOP_K + k + 1].astype(F32)
        cols = jnp.where(lane == k, slot, cols)
    dest_ref[...] = cols.T[:TOP_K, :].astype(jnp.int32)


def _slots(ei, first_row, tm):
    n, wc = ei.shape
    tm = min(tm, n)
    return pl.pallas_call(
        _slot_kernel,
        grid=(n // tm,),
        in_specs=[
            pl.BlockSpec((tm, wc), lambda i: (i, 0)),
            pl.BlockSpec((1, wc), lambda i: (0, 0)),
        ],
        out_specs=pl.BlockSpec((TOP_K, tm), lambda i: (0, i)),
        out_shape=jax.ShapeDtypeStruct((TOP_K, n), jnp.int32),
        compiler_params=_params("parallel"),
        name="moe_slots",
    )(ei, first_row)


def _dispatch_kernel(dest_ref, last_ref, nv_ref, hn_ref, x_ref, zero_ref, zsem, tsem, rsem,
                     *, n_experts, n_blocks, tb, nw):
    i = pl.program_id(0)
    tm = hn_ref.shape[0] // nw
    n_tokens = pl.num_programs(0) * tm

    def row(ref, r):
        return ref.at[pl.ds(pl.multiple_of(r * nw, nw), nw), :]

    def zero_copy(block, sem):
        line0 = pl.multiple_of(jnp.maximum(block, 0) * (tb * nw), tb * nw)
        return pltpu.make_async_copy(zero_ref, x_ref.at[pl.ds(line0, tb * nw), :], sem)

    def for_expert_tails(fn):
        def body(e, c):
            @pl.when(last_ref[e] >= 0)
            def _():
                fn(zero_copy(last_ref[e], zsem.at[e]))
            return c
        lax.fori_loop(0, n_experts, body, 0)

    def for_unused_blocks(fn):
        def body(b, c):
            @pl.when(b >= nv_ref[0])
            def _():
                fn(zero_copy(b, tsem))
            return c
        lax.fori_loop(0, n_blocks, body, 0)

    @pl.when(i == 0)
    def _():
        zero_ref[...] = jnp.zeros_like(zero_ref)
        for_expert_tails(lambda cp: cp.start())
        for_unused_blocks(lambda cp: cp.start())
        for_expert_tails(lambda cp: cp.wait())
        for_unused_blocks(lambda cp: cp.wait())

    def start_one(j, c):
        for k in range(TOP_K):
            dst = dest_ref[k * n_tokens + i * tm + j]
            pltpu.make_async_copy(row(hn_ref, j), row(x_ref, dst), rsem).start()
        return c

    def wait_one(j, c):
        pltpu.make_async_copy(row(hn_ref, 0), row(x_ref, 0), rsem).wait()
        return c

    lax.fori_loop(0, tm, start_one, 0, unroll=4)
    lax.fori_loop(0, TOP_K * tm, wait_one, 0, unroll=8)


def _dispatch(hn_rows, dest, last_block, n_valid, n_blocks, nw, tm):
    n = hn_rows.shape[0] // nw
    tm = min(tm, n)
    n_experts = last_block.shape[0]
    tb = DISPATCH_BLOCK
    kern = functools.partial(_dispatch_kernel, n_experts=n_experts, n_blocks=n_blocks, tb=tb, nw=nw)
    grid_spec = pltpu.PrefetchScalarGridSpec(
        num_scalar_prefetch=3,
        grid=(n // tm,),
        in_specs=[pl.BlockSpec((tm * nw, LANES), lambda i, dest, last, nv: (i, 0))],
        out_specs=pl.BlockSpec(memory_space=pl.ANY),
        scratch_shapes=[
            pltpu.VMEM((tb * nw, LANES), jnp.uint32),
            pltpu.SemaphoreType.DMA((n_experts,)),
            pltpu.SemaphoreType.DMA,
            pltpu.SemaphoreType.DMA,
        ],
    )
    return pl.pallas_call(
        kern,
        grid_spec=grid_spec,
        out_shape=jax.ShapeDtypeStruct((n_blocks * tb * nw, LANES), jnp.uint32),
        compiler_params=_params("arbitrary"),
        name="moe_dispatch",
    )(dest, last_block, n_valid, hn_rows)


def _expert_kernel(be_ref, nv_ref, seg_ref, nxt_ref, x_ref, wg_hbm, wu_hbm, wd_hbm, o_ref,
                   wg_buf, wu_buf, wd_buf, wgb_ref, wub_ref, wdb_ref, sem):
    i = pl.program_id(0)
    valid = i < nv_ref[0]
    new_expert = jnp.logical_or(i == 0, be_ref[i] != be_ref[jnp.maximum(i - 1, 0)])
    slot = seg_ref[i] % 2

    def weight_copies(e, s):
        return (pltpu.make_async_copy(wg_hbm.at[e], wg_buf.at[s], sem.at[s, 0]),
                pltpu.make_async_copy(wu_hbm.at[e], wu_buf.at[s], sem.at[s, 1]),
                pltpu.make_async_copy(wd_hbm.at[e], wd_buf.at[s], sem.at[s, 2]))

    @pl.when(i == 0)
    def _():
        for cp in weight_copies(be_ref[0], 0):
            cp.start()

    @pl.when(jnp.logical_and(valid, new_expert))
    def _():
        @pl.when(nxt_ref[i] >= 0)
        def _():
            for cp in weight_copies(nxt_ref[i], 1 - slot):
                cp.start()

        for cp in weight_copies(be_ref[i], slot):
            cp.wait()
        wgb_ref[...] = wg_buf[slot].astype(BF16)
        wub_ref[...] = wu_buf[slot].astype(BF16)
        wdb_ref[...] = wd_buf[slot].astype(BF16)

    @pl.when(valid)
    def _():
        half = wgb_ref.shape[0] // 2
        nw = half // LANES
        x_lo, x_hi = _unpack_halves(_load_rows(x_ref, 0, x_ref.shape[0] // nw, nw))
        x_lo, x_hi = x_lo.astype(BF16), x_hi.astype(BF16)

        def proj(w_ref):
            return (jnp.dot(x_lo, w_ref[:half, :], preferred_element_type=F32)
                    + jnp.dot(x_hi, w_ref[half:, :], preferred_element_type=F32))

        hid = jax.nn.silu(proj(wgb_ref)) * proj(wub_ref)
        _store_rows(o_ref, _pack_halves(jnp.dot(hid.astype(BF16), wdb_ref[...], preferred_element_type=F32)))

    @pl.when(jnp.logical_not(valid))
    def _():
        o_ref[...] = jnp.zeros_like(o_ref)


def _experts(x_disp, block_expert, n_valid, block_segment, next_expert, w_gate, w_up, w_down):
    d, de = w_gate.shape[1], w_gate.shape[2]
    nw = d // (2 * LANES)
    rows = x_disp.shape[0] // nw
    tb = DISPATCH_BLOCK
    grid_spec = pltpu.PrefetchScalarGridSpec(
        num_scalar_prefetch=4,
        grid=(rows // tb,),
        in_specs=[
            pl.BlockSpec((tb * nw, LANES),
                         lambda i, be, nv, seg, nxt: (jnp.maximum(jnp.minimum(i, nv[0] - 1), 0), 0)),
            pl.BlockSpec(memory_space=pl.ANY),
            pl.BlockSpec(memory_space=pl.ANY),
            pl.BlockSpec(memory_space=pl.ANY),
        ],
        out_specs=pl.BlockSpec((tb * nw, LANES), lambda i, be, nv, seg, nxt: (i, 0)),
        scratch_shapes=[
            pltpu.VMEM((2, d, de), w_gate.dtype),
            pltpu.VMEM((2, d, de), w_up.dtype),
            pltpu.VMEM((2, de, d), w_down.dtype),
            pltpu.VMEM((d, de), BF16),
            pltpu.VMEM((d, de), BF16),
            pltpu.VMEM((de, d), BF16),
            pltpu.SemaphoreType.DMA((2, 3)),
        ],
    )
    return pl.pallas_call(
        _expert_kernel,
        grid_spec=grid_spec,
        out_shape=jax.ShapeDtypeStruct((rows * nw, LANES), jnp.uint32),
        compiler_params=_params("arbitrary"),
        name="moe_experts",
    )(block_expert, n_valid, block_segment, next_expert, x_disp, w_gate, w_up, w_down)


def _combine_kernel(dest_ref, h_ref, ew_ref, g_ref, y_ref, o_ref, buf_ref, sem, *, final_norm):
    tm, d = h_ref.shape
    nw = d // (2 * LANES)
    i = pl.program_id(0)
    n_tiles = pl.num_programs(0)

    def row(ref, r):
        return ref.at[pl.ds(pl.multiple_of(r * nw, nw), nw), :]

    def start_tile(tile, slot):
        def start_one(j, c):
            for k in range(TOP_K):
                src = dest_ref[k * (n_tiles * tm) + tile * tm + j]
                pltpu.make_async_copy(row(y_ref, src), row(buf_ref.at[slot], k * tm + j), sem.at[slot]).start()
            return c

        lax.fori_loop(0, tm, start_one, 0, unroll=4)

    def wait_tile(slot):
        def wait_one(j, c):
            pltpu.make_async_copy(row(y_ref, 0), row(buf_ref.at[slot], 0), sem.at[slot]).wait()
            return c

        lax.fori_loop(0, TOP_K * tm, wait_one, 0, unroll=8)

    slot = i % 2

    @pl.when(i == 0)
    def _():
        start_tile(0, 0)

    @pl.when(i + 1 < n_tiles)
    def _():
        start_tile(i + 1, 1 - slot)

    wait_tile(slot)
    h = h_ref[...]
    w = ew_ref[...]
    lo, hi = h[:, :d // 2], h[:, d // 2:]
    for k in range(TOP_K):
        y_lo, y_hi = _unpack_halves(_load_rows(buf_ref.at[slot], k * tm, tm, nw))
        lo = lo + y_lo * w[:, k:k + 1]
        hi = hi + y_hi * w[:, k:k + 1]
    if final_norm:
        ms = (jnp.sum(lo * lo, axis=-1, keepdims=True) + jnp.sum(hi * hi, axis=-1, keepdims=True)) / d
        scale = lax.rsqrt(ms + RMS_EPS)
        lo = lo * scale * g_ref[:, :d // 2]
        hi = hi * scale * g_ref[:, d // 2:]
    o_ref[:, :d // 2] = lo
    o_ref[:, d // 2:] = hi


def _combine(h2, y_disp, dest, ew, gain, final_norm, tm):
    n, d = h2.shape
    tm = min(tm, n)
    nw = d // (2 * LANES)
    wc = ew.shape[1]
    kern = functools.partial(_combine_kernel, final_norm=final_norm)
    grid_spec = pltpu.PrefetchScalarGridSpec(
        num_scalar_prefetch=1,
        grid=(n // tm,),
        in_specs=[
            pl.BlockSpec((tm, d), lambda i, dest: (i, 0)),
            pl.BlockSpec((tm, wc), lambda i, dest: (i, 0)),
            pl.BlockSpec((1, d), lambda i, dest: (0, 0)),
            pl.BlockSpec(memory_space=pl.ANY),
        ],
        out_specs=pl.BlockSpec((tm, d), lambda i, dest: (i, 0)),
        scratch_shapes=[
            pltpu.VMEM((2, TOP_K * tm * nw, LANES), jnp.uint32),
            pltpu.SemaphoreType.DMA((2,)),
        ],
    )
    return pl.pallas_call(
        kern,
        grid_spec=grid_spec,
        out_shape=jax.ShapeDtypeStruct((n, d), F32),
        compiler_params=_params("arbitrary"),
        name="moe_combine",
    )(dest, h2, ew, gain.reshape(1, d), y_disp)


def _hier_moe(h2, norm_gain, w_group, b_group, w_expert, b_expert, w_gate, w_up, w_down, final_gain):
    n, d = h2.shape
    n_groups = w_group.shape[1]
    n_experts = w_expert.shape[1]
    epg = n_experts // n_groups
    n_logits = n_groups + n_experts
    wc = -(-n_logits // LANES) * LANES
    w_router = jnp.concatenate(
        [w_group.astype(F32), w_expert.astype(F32), jnp.zeros((d, wc - n_logits), F32)], axis=1)
    b_router = jnp.concatenate(
        [b_group.astype(F32), b_expert.astype(F32), jnp.zeros((wc - n_logits,), F32)]).reshape(1, wc)
    hn_packed, ei, ew, counts = _router(h2, norm_gain, w_router, b_router, n_groups, epg, tm=512)

    tb = DISPATCH_BLOCK
    n_pairs = n * TOP_K
    counts = counts[0, :n_experts]
    blocks_per_expert = (counts + tb - 1) // tb
    block_end = jnp.cumsum(blocks_per_expert)
    block_start = block_end - blocks_per_expert
    first_row = jnp.pad((block_start * tb).astype(F32), (0, wc - n_experts)).reshape(1, wc)
    dest = _slots(ei, first_row, tm=512).reshape(-1)
    last_block = jnp.where(blocks_per_expert > 0, block_end - 1, -1).astype(jnp.int32)
    n_blocks = -(-n_pairs // tb) + n_experts
    n_valid = block_end[-1:].astype(jnp.int32)
    blk = jnp.minimum(jnp.arange(n_blocks, dtype=jnp.int32), n_valid[0] - 1)
    block_expert = jnp.searchsorted(block_end, blk, side="right").astype(jnp.int32)
    expert_range = jnp.arange(n_experts, dtype=jnp.int32)
    owners = jnp.where(blocks_per_expert > 0, expert_range, n_experts)
    owner_at_or_after = lax.cummin(owners, axis=0, reverse=True)
    owner_after = jnp.concatenate([owner_at_or_after[1:], jnp.full((1,), n_experts, jnp.int32)])
    next_expert = jnp.take(jnp.where(owner_after < n_experts, owner_after, -1), block_expert).astype(jnp.int32)
    changed = jnp.concatenate([jnp.zeros((1,), jnp.int32),
                               (block_expert[1:] != block_expert[:-1]).astype(jnp.int32)])
    block_segment = jnp.cumsum(changed).astype(jnp.int32)

    x_disp = _dispatch(hn_packed, dest, last_block, n_valid, n_blocks, d // (2 * LANES), tm=512)
    y_disp = _experts(x_disp, block_expert, n_valid, block_segment, next_expert, w_gate, w_up, w_down)
    gain = norm_gain if final_gain is None else final_gain
    return _combine(h2, y_disp, dest, ew, gain, final_gain is not None, tm=512)


def _forget_kernel(f_ref, b_ref, c_ref):
    heads = c_ref.shape[0]
    x = jax.nn.log_sigmoid(f_ref[...].T[:heads, :] + b_ref[...])
    s = x.shape[1]
    lane = lax.broadcasted_iota(jnp.int32, x.shape, 1)
    shift = 1
    while shift < s:
        x = x + jnp.where(lane >= shift, pltpu.roll(x, shift, axis=1), 0.0)
        shift *= 2
    c_ref[...] = x


def _forget_cumsum(f_logit, b_forget, b, s):
    h = b_forget.shape[0]
    return pl.pallas_call(
        _forget_kernel,
        grid=(b,),
        in_specs=[
            pl.BlockSpec((s, f_logit.shape[1]), lambda i: (i, 0)),
            pl.BlockSpec((h, 1), lambda i: (0, 0)),
        ],
        out_specs=pl.BlockSpec((None, h, s), lambda i: (i, 0, 0)),
        out_shape=jax.ShapeDtypeStruct((b, h, s), F32),
        compiler_params=_params("parallel"),
        name="fox_forget",
    )(f_logit, b_forget.astype(F32).reshape(h, 1))


def _fox_attn_kernel(q_ref, k_ref, v_ref, c_ref, o_ref, *, tk, n_sub):
    seq, dh = q_ref.shape
    head = pl.program_id(1)
    log2e = math.log2(math.e)
    row = lax.broadcasted_iota(jnp.int32, (tk, tk), 0)
    col = lax.broadcasted_iota(jnp.int32, (tk, tk), 1)
    causal = col <= row

    def attend(q, k, v, bias, carry, mask):
        m, l, acc = carry
        s = lax.dot_general(q, k, (((1,), (1,)), ((), ())), preferred_element_type=F32) + bias
        if mask:
            s = jnp.where(causal, s, -jnp.inf)
        m_new = jnp.maximum(m, jnp.max(s, axis=-1, keepdims=True))
        alpha = jnp.exp2(m - m_new)
        p = jnp.exp2(s - m_new)
        l = alpha * l + jnp.sum(p, axis=-1, keepdims=True)
        acc = alpha * acc + jnp.dot(p.astype(BF16), v, preferred_element_type=F32)
        return m_new, l, acc

    def key_block(k0):
        bias = c_ref[pl.ds(head, 1), pl.ds(k0, tk)] * (-log2e)
        return k_ref[pl.ds(k0, tk), :], v_ref[pl.ds(k0, tk), :], bias

    for qt in range(seq // (tk * n_sub)):
        first = qt * n_sub
        qs = [q_ref[(first + a) * tk:(first + a + 1) * tk, :] for a in range(n_sub)]
        init = (jnp.full((tk, 1), -jnp.inf, F32), jnp.zeros((tk, 1), F32), jnp.zeros((tk, dh), F32))
        carries = (init,) * n_sub

        def kv_step(ki, carries, qs=qs):
            k, v, bias = key_block(pl.multiple_of(ki * tk, tk))
            return tuple(attend(qs[a], k, v, bias, carries[a], False) for a in range(n_sub))

        if qt > 0:
            carries = lax.fori_loop(0, first, kv_step, carries)
        carries = list(carries)
        for kb in range(n_sub):
            k, v, bias = key_block((first + kb) * tk)
            for a in range(kb, n_sub):
                carries[a] = attend(qs[a], k, v, bias, carries[a], a == kb)
        for a in range(n_sub):
            _, l, acc = carries[a]
            o_ref[(first + a) * tk:(first + a + 1) * tk, :] = (acc / l).astype(o_ref.dtype)


def _fox_attention(proj, cum, batch, seq, heads, tk, n_sub):
    dh = FOX_HEAD_DIM
    tk = min(tk, seq)
    n_sub = min(n_sub, seq // tk)
    kern = functools.partial(_fox_attn_kernel, tk=tk, n_sub=n_sub)
    return pl.pallas_call(
        kern,
        grid=(batch, heads),
        in_specs=[
            pl.BlockSpec((seq, dh), lambda b, h: (b, h)),
            pl.BlockSpec((seq, dh), lambda b, h: (b, heads + h)),
            pl.BlockSpec((seq, dh), lambda b, h: (b, 2 * heads + h)),
            pl.BlockSpec((None, heads, seq), lambda b, h: (b, 0, 0)),
        ],
        out_specs=pl.BlockSpec((seq, dh), lambda b, h: (b, h)),
        out_shape=jax.ShapeDtypeStruct((batch * seq, heads * dh), BF16),
        compiler_params=_params("parallel", "parallel"),
        name="fox_attention",
    )(proj, proj, proj, cum)


def _fox_out_kernel(o_ref, gate_ref, w_ref, h_ref, out_ref):
    gated = (o_ref[...].astype(F32) * jax.nn.sigmoid(gate_ref[...].astype(F32))).astype(BF16)
    out_ref[...] = h_ref[...] + jnp.dot(gated, w_ref[...], preferred_element_type=F32)


def _fox_out(o, proj, w_out, h2, tm):
    n, d = h2.shape
    tm = min(tm, n)
    gate_block = 3
    return pl.pallas_call(
        _fox_out_kernel,
        grid=(n // tm,),
        in_specs=[
            pl.BlockSpec((tm, d), lambda i: (i, 0)),
            pl.BlockSpec((tm, d), lambda i: (i, gate_block)),
            pl.BlockSpec((d, d), lambda i: (0, 0)),
            pl.BlockSpec((tm, d), lambda i: (i, 0)),
        ],
        out_specs=pl.BlockSpec((tm, d), lambda i: (i, 0)),
        out_shape=jax.ShapeDtypeStruct((n, d), F32),
        compiler_params=_params("parallel"),
        name="fox_out",
    )(o, proj, w_out, h2)


def _s5_layer(x, norm_gain, w_in, lam_re, lam_im, b_re, b_im, c_re, c_im, d_skip, log_step, w_glu, b_glu, w_out):
    b, s, d = x.shape
    assert b == STATE_BATCH
    ds = w_in.shape[1]
    bmat, cmat, a_re, a_im = _s5_discretise(lam_re, lam_im, b_re, b_im, c_re, c_im, log_step)
    u_tm = _s5_in(x, norm_gain, w_in.astype(BF16), ts=64)
    y_tm = _s5_scan(u_tm, bmat, cmat, a_re, a_im, d_skip.astype(F32), t_chunk=128)
    return _s5_out(y_tm, x, w_glu.astype(BF16), b_glu.astype(F32), w_out.astype(BF16), ts=64)


def _fox_layer(h, norm_gain, w_in, b_forget, w_out):
    b, s, d = h.shape
    heads = d // FOX_HEAD_DIM
    w_all = w_in.astype(BF16)
    w_forget = jnp.pad(w_in[:, 4 * d:], ((0, 0), (0, LANES - heads))).astype(BF16)
    q_scale = FOX_HEAD_DIM ** -0.5 * math.log2(math.e)
    col_scale = jnp.concatenate([jnp.full((d,), q_scale, F32), jnp.ones((3 * d,), F32)])
    proj, f_logit = _rms_matmul(h, norm_gain, w_all, 4 * d, col_scale, BF16, ts=1024, tn=1024, w_side=w_forget)
    cum = _forget_cumsum(f_logit, b_forget, b, s)
    o = _fox_attention(proj, cum, b, s, heads, tk=512, n_sub=4)
    return _fox_out(o, proj, w_out.astype(BF16), h.reshape(b * s, d), tm=512).reshape(b, s, d)


def kernel(x, l0_mix_norm, l0_s5_w_in, l0_s5_lambda_re, l0_s5_lambda_im, l0_s5_b_re, l0_s5_b_im, l0_s5_c_re, l0_s5_c_im, l0_s5_d, l0_s5_log_step, l0_s5_w_glu, l0_s5_b_glu, l0_s5_w_out, l0_ffn_norm, l0_moe_w_group, l0_moe_b_group, l0_moe_w_expert, l0_moe_b_expert, l0_moe_w_gate, l0_moe_w_up, l0_moe_w_down, l1_mix_norm, l1_fox_w_in, l1_fox_b_forget, l1_fox_w_out, l1_ffn_norm, l1_moe_w_group, l1_moe_b_group, l1_moe_w_expert, l1_moe_b_expert, l1_moe_w_gate, l1_moe_w_up, l1_moe_w_down, final_norm):
    b, s, d = x.shape
    h = _s5_layer(x, l0_mix_norm, l0_s5_w_in, l0_s5_lambda_re, l0_s5_lambda_im, l0_s5_b_re, l0_s5_b_im,
                  l0_s5_c_re, l0_s5_c_im, l0_s5_d, l0_s5_log_step, l0_s5_w_glu, l0_s5_b_glu, l0_s5_w_out)
    h = _hier_moe(h.reshape(b * s, d), l0_ffn_norm, l0_moe_w_group, l0_moe_b_group, l0_moe_w_expert,
                  l0_moe_b_expert, l0_moe_w_gate, l0_moe_w_up, l0_moe_w_down, None).reshape(b, s, d)
    h = _fox_layer(h, l1_mix_norm, l1_fox_w_in, l1_fox_b_forget, l1_fox_w_out)
    h = _hier_moe(h.reshape(b * s, d), l1_ffn_norm, l1_moe_w_group, l1_moe_b_group, l1_moe_w_expert,
                  l1_moe_b_expert, l1_moe_w_gate, l1_moe_w_up, l1_moe_w_down, final_norm)
    return h.reshape(b, s, d)
```

```python
import functools
import math

import jax
import jax.numpy as jnp
from jax import lax
from jax.experimental import pallas as pl
from jax.experimental.pallas import tpu as pltpu

F32 = jnp.float32
BF16 = jnp.bfloat16

RMS_EPS = 1e-6
STATE_BATCH = 8
SSM_GROUPS_PER_BLOCK = 8
FOX_HEAD_DIM = 128
TOP_K = 2
DISPATCH_BLOCK = 256
WEIGHT_LOOKAHEAD = 2
LANES = 128
VMEM_LIMIT = 56 * 1024 * 1024


def _params(*sem):
    return pltpu.CompilerParams(dimension_semantics=sem, vmem_limit_bytes=VMEM_LIMIT)


def _rms(x, gain):
    ms = jnp.mean(x * x, axis=-1, keepdims=True)
    return x * lax.rsqrt(ms + RMS_EPS) * gain


def _s5_in_kernel(x_ref, g_ref, w_ref, u_ref):
    nb, ts, d = x_ref.shape
    nl = w_ref.shape[1] // LANES
    xn = _rms(x_ref[...].reshape(nb * ts, d), g_ref[...]).astype(BF16)
    u = jnp.dot(xn, w_ref[...], preferred_element_type=F32)
    for bi in range(nb):
        for c in range(nl):
            u_ref[pl.ds(bi * nl + c, ts, stride=nb * nl), :] = u[bi * ts:(bi + 1) * ts, c * LANES:(c + 1) * LANES]


def _s5_in(x, gain, w, ts):
    b, s, d = x.shape
    m = w.shape[1]
    nl = m // LANES
    ts = min(ts, s)
    return pl.pallas_call(
        _s5_in_kernel,
        grid=(s // ts,),
        in_specs=[
            pl.BlockSpec((b, ts, d), lambda si: (0, si, 0)),
            pl.BlockSpec((1, d), lambda si: (0, 0)),
            pl.BlockSpec((d, m), lambda si: (0, 0)),
        ],
        out_specs=pl.BlockSpec((ts * b * nl, LANES), lambda si: (si, 0)),
        out_shape=jax.ShapeDtypeStruct((s * b * nl, LANES), F32),
        compiler_params=_params("parallel"),
        name="s5_in",
    )(x, gain.reshape(1, d), w)


def _s5_scan_kernel(u_ref, bmat_ref, cmat_ref, are_ref, aim_ref, d_ref, y_ref, bu2_ref, st_ref, *, n_blocks, t_chunk):
    @pl.when(pl.program_id(0) == 0)
    def _():
        st_ref[...] = jnp.zeros_like(st_ref)

    cw = bmat_ref.shape[1]
    sw = bmat_ref.shape[2] // 2
    rows = t_chunk * STATE_BATCH
    for gb in range(n_blocks):
        u_blk = u_ref[pl.ds(gb, rows, stride=n_blocks), :]
        bu_ref = bu2_ref.at[gb % 2]
        bu_ref[...] = jnp.dot(u_blk.astype(BF16), bmat_ref[gb], preferred_element_type=F32)
        a_re = jnp.broadcast_to(are_ref[gb], (STATE_BATCH, sw))
        a_im = jnp.broadcast_to(aim_ref[gb], (STATE_BATCH, sw))

        def step(t, carry):
            s_re, s_im = carry
            r0 = pl.multiple_of(t * STATE_BATCH, STATE_BATCH)
            b_re = bu_ref[pl.ds(r0, STATE_BATCH), 0:sw]
            b_im = bu_ref[pl.ds(r0, STATE_BATCH), sw:2 * sw]
            n_re = a_re * s_re - a_im * s_im + b_re
            n_im = a_re * s_im + a_im * s_re + b_im
            bu_ref[pl.ds(r0, STATE_BATCH), 0:sw] = n_re
            bu_ref[pl.ds(r0, STATE_BATCH), sw:2 * sw] = n_im
            return n_re, n_im

        s_re, s_im = lax.fori_loop(0, t_chunk, step, (st_ref[gb, 0], st_ref[gb, 1]), unroll=True)
        st_ref[gb, 0] = s_re
        st_ref[gb, 1] = s_im
        y = jnp.dot(bu_ref[...].astype(BF16), cmat_ref[gb], preferred_element_type=F32)
        y_ref[pl.ds(gb, rows, stride=n_blocks), :] = y + d_ref[:, gb * cw:(gb + 1) * cw] * u_blk


def _s5_scan(u_tm, bmat, cmat, a_re, a_im, d_skip, t_chunk):
    nb, cw, sw2 = bmat.shape
    assert cw == LANES
    ds = nb * cw
    seq = u_tm.shape[0] // (STATE_BATCH * nb)
    t_chunk = min(t_chunk, seq)
    tr = t_chunk * STATE_BATCH
    kern = functools.partial(_s5_scan_kernel, n_blocks=nb, t_chunk=t_chunk)
    return pl.pallas_call(
        kern,
        grid=(seq // t_chunk,),
        in_specs=[
            pl.BlockSpec((tr * nb, LANES), lambda i: (i, 0)),
            pl.BlockSpec((nb, cw, sw2), lambda i: (0, 0, 0)),
            pl.BlockSpec((nb, sw2, cw), lambda i: (0, 0, 0)),
            pl.BlockSpec((nb, 1, sw2 // 2), lambda i: (0, 0, 0)),
            pl.BlockSpec((nb, 1, sw2 // 2), lambda i: (0, 0, 0)),
            pl.BlockSpec((1, ds), lambda i: (0, 0)),
        ],
        out_specs=pl.BlockSpec((tr * nb, LANES), lambda i: (i, 0)),
        out_shape=jax.ShapeDtypeStruct(u_tm.shape, F32),
        scratch_shapes=[
            pltpu.VMEM((2, tr, sw2), F32),
            pltpu.VMEM((nb, 2, STATE_BATCH, sw2 // 2), F32),
        ],
        compiler_params=_params("arbitrary"),
        name="s5_scan",
    )(u_tm, bmat, cmat, a_re, a_im, d_skip.reshape(1, ds))


def _s5_discretise(lam_re, lam_im, b_re, b_im, c_re, c_im, log_step):
    lr, li = lam_re.astype(F32), lam_im.astype(F32)
    step = jnp.exp(log_step.astype(F32))[:, None]
    mag = jnp.exp(lr * step)
    ab_re = mag * jnp.cos(li * step)
    ab_im = mag * jnp.sin(li * step)
    den = lr * lr + li * li
    nr, ni = ab_re - 1.0, ab_im
    fr = (nr * lr + ni * li) / den
    fi = (ni * lr - nr * li) / den
    br, bi = b_re.astype(F32), b_im.astype(F32)
    bb_re = fr[..., None] * br - fi[..., None] * bi
    bb_im = fr[..., None] * bi + fi[..., None] * br
    g, n, c = bb_re.shape
    gpb = min(SSM_GROUPS_PER_BLOCK, g)
    nb = g // gpb
    eye = jnp.eye(gpb, dtype=F32)

    def in_blocks(bb):
        return jnp.einsum("bjnc,jk->bjckn", bb.reshape(nb, gpb, n, c), eye).reshape(nb, gpb * c, gpb * n)

    def out_blocks(cc):
        return jnp.einsum("bjcn,jk->bjnkc", cc.reshape(nb, gpb, c, n), eye).reshape(nb, gpb * n, gpb * c)

    bmat = jnp.concatenate([in_blocks(bb_re), in_blocks(bb_im)], axis=-1).astype(BF16)
    cmat = jnp.concatenate([out_blocks(c_re.astype(F32)), -out_blocks(c_im.astype(F32))], axis=1).astype(BF16)
    return bmat, cmat, ab_re.reshape(nb, 1, gpb * n), ab_im.reshape(nb, 1, gpb * n)


def _s5_out_kernel(y_ref, x_ref, wg_ref, bg_ref, wo_ref, o_ref):
    nb, ts, d = x_ref.shape
    nl = wg_ref.shape[0] // LANES
    y = jnp.concatenate(
        [jnp.concatenate([y_ref[pl.ds(bi * nl + c, ts, stride=nb * nl), :] for c in range(nl)], axis=1)
         for bi in range(nb)], axis=0)
    g = jax.nn.gelu(y)
    z = jnp.dot(g.astype(BF16), wg_ref[...], preferred_element_type=F32) + bg_ref[...]
    gated = g * jax.nn.sigmoid(z)
    out = jnp.dot(gated.astype(BF16), wo_ref[...], preferred_element_type=F32)
    o_ref[...] = x_ref[...] + out.reshape(nb, ts, d)


def _s5_out(y_tm, x, w_glu, b_glu, w_out, ts):
    b, s, d = x.shape
    ds = w_glu.shape[0]
    nl = ds // LANES
    ts = min(ts, s)
    return pl.pallas_call(
        _s5_out_kernel,
        grid=(s // ts,),
        in_specs=[
            pl.BlockSpec((ts * b * nl, LANES), lambda si: (si, 0)),
            pl.BlockSpec((b, ts, d), lambda si: (0, si, 0)),
            pl.BlockSpec((ds, ds), lambda si: (0, 0)),
            pl.BlockSpec((1, ds), lambda si: (0, 0)),
            pl.BlockSpec((ds, d), lambda si: (0, 0)),
        ],
        out_specs=pl.BlockSpec((b, ts, d), lambda si: (0, si, 0)),
        out_shape=jax.ShapeDtypeStruct((b, s, d), F32),
        compiler_params=_params("parallel"),
        name="s5_out",
    )(y_tm, x, w_glu, b_glu.reshape(1, ds), w_out)


def _pack_halves(x):
    half = x.shape[-1] // 2
    lo = lax.bitcast_convert_type(x[:, :half].astype(BF16).astype(F32), jnp.uint32)
    hi = lax.bitcast_convert_type(x[:, half:].astype(BF16).astype(F32), jnp.uint32)
    return (lo >> 16) | (hi & jnp.uint32(0xFFFF0000))


def _unpack_halves(words):
    lo = lax.bitcast_convert_type(words << 16, F32)
    hi = lax.bitcast_convert_type(words & jnp.uint32(0xFFFF0000), F32)
    return lo, hi


def _store_rows(ref, words):
    rows, nw = words.shape[0], words.shape[1] // LANES
    for c in range(nw):
        ref[pl.ds(c, rows, stride=nw), :] = words[:, c * LANES:(c + 1) * LANES]


def _load_rows(ref, first_row, rows, nw):
    return jnp.concatenate(
        [ref[pl.ds(first_row * nw + c, rows, stride=nw), :] for c in range(nw)], axis=1)


def _router_kernel(h_ref, g_ref, whi_ref, wlo_ref, b_ref, tri_ref, hn_ref, ei_ref, ew_ref, cnt_ref, base_ref,
                   *, n_groups, epg):
    @pl.when(pl.program_id(0) == 0)
    def _():
        base_ref[...] = jnp.zeros_like(base_ref)

    hn = _rms(h_ref[...], g_ref[...])
    _store_rows(hn_ref, _pack_halves(hn))
    hn_hi = hn.astype(BF16)
    hn_lo = (hn - hn_hi.astype(F32)).astype(BF16)
    logits = (jnp.dot(hn_hi, whi_ref[...], preferred_element_type=F32)
              + jnp.dot(hn_lo, whi_ref[...], preferred_element_type=F32)
              + jnp.dot(hn_hi, wlo_ref[...], preferred_element_type=F32)) + b_ref[...]
    lane = lax.broadcasted_iota(jnp.int32, logits.shape, 1)
    n_lanes = logits.shape[1]

    def first_argmax(vals, vmax):
        first = jnp.min(jnp.where(vals == vmax, lane, n_lanes).astype(F32), axis=-1, keepdims=True)
        return first.astype(jnp.int32)

    in_groups = lane < n_groups
    gl = jnp.where(in_groups, logits, -jnp.inf)
    g_max = jnp.max(gl, axis=-1, keepdims=True)
    g_sel = first_argmax(gl, g_max)
    p_group = 1.0 / jnp.sum(jnp.where(in_groups, jnp.exp(logits - g_max), 0.0), axis=-1, keepdims=True)

    lo = n_groups + g_sel * epg
    el = jnp.where(jnp.logical_and(lane >= lo, lane < lo + epg), logits, -jnp.inf)
    t1 = jnp.max(el, axis=-1, keepdims=True)
    i1 = first_argmax(el, t1)
    el2 = jnp.where(lane == i1, -jnp.inf, el)
    t2 = jnp.max(el2, axis=-1, keepdims=True)
    i2 = first_argmax(el2, t2)
    e2x = jnp.exp(t2 - t1)
    w1 = p_group / (1.0 + e2x)
    w2 = p_group * e2x / (1.0 + e2x)
    e1 = i1 - n_groups
    e2 = i2 - n_groups

    onehot = jnp.logical_or(lane == e1, lane == e2)
    earlier = jnp.dot(tri_ref[...], jnp.where(onehot, 1.0, 0.0).astype(BF16), preferred_element_type=F32)
    total = earlier + base_ref[...]
    r1 = jnp.sum(jnp.where(lane == e1, total, 0.0), axis=-1, keepdims=True).astype(jnp.int32)
    r2 = jnp.sum(jnp.where(lane == e2, total, 0.0), axis=-1, keepdims=True).astype(jnp.int32)
    base_ref[...] = base_ref[...] + jnp.sum(jnp.where(onehot, 1.0, 0.0), axis=0, keepdims=True)
    cnt_ref[...] = base_ref[...].astype(jnp.int32)

    ei_ref[...] = jnp.where(lane == 0, e1, jnp.where(lane == 1, e2, jnp.where(lane == 2, r1, jnp.where(lane == 3, r2, 0))))
    ew_ref[...] = jnp.where(lane == 0, w1, jnp.where(lane == 1, w2, 0.0))


def _router(h2, gain, w_router, b_router, n_groups, epg, tm):
    n, d = h2.shape
    tm = min(tm, n)
    wc = w_router.shape[1]
    nw = d // (2 * LANES)
    w_hi = w_router.astype(BF16)
    w_lo = (w_router - w_hi.astype(F32)).astype(BF16)
    tri = jnp.tri(tm, k=-1, dtype=BF16)
    kern = functools.partial(_router_kernel, n_groups=n_groups, epg=epg)
    return pl.pallas_call(
        kern,
        grid=(n // tm,),
        in_specs=[
            pl.BlockSpec((tm, d), lambda i: (i, 0)),
            pl.BlockSpec((1, d), lambda i: (0, 0)),
            pl.BlockSpec((d, wc), lambda i: (0, 0)),
            pl.BlockSpec((d, wc), lambda i: (0, 0)),
            pl.BlockSpec((1, wc), lambda i: (0, 0)),
            pl.BlockSpec((tm, tm), lambda i: (0, 0)),
        ],
        out_specs=[
            pl.BlockSpec((tm * nw, LANES), lambda i: (i, 0)),
            pl.BlockSpec((tm, wc), lambda i: (i, 0)),
            pl.BlockSpec((tm, wc), lambda i: (i, 0)),
            pl.BlockSpec((1, wc), lambda i: (0, 0)),
        ],
        out_shape=[
            jax.ShapeDtypeStruct((n * nw, LANES), jnp.uint32),
            jax.ShapeDtypeStruct((n, wc), jnp.int32),
            jax.ShapeDtypeStruct((n, wc), F32),
            jax.ShapeDtypeStruct((1, wc), jnp.int32),
        ],
        scratch_shapes=[pltpu.VMEM((1, wc), F32)],
        compiler_params=_params("arbitrary"),
        name="moe_router",
    )(h2, gain.reshape(1, d), w_hi, w_lo, b_router, tri)


def _slot_kernel(ei_ref, first_ref, dest_ref):
    ei = ei_ref[...]
    lane = lax.broadcasted_iota(jnp.int32, ei.shape, 1)
    cols = jnp.zeros(ei.shape, F32)
    for k in range(TOP_K):
        first = jnp.sum(jnp.where(lane == ei[:, k:k + 1], first_ref[...], 0.0), axis=-1, keepdims=True)
        slot = first + ei[:, TOP_K + k:TOP_K + k + 1].astype(F32)
        cols = jnp.where(lane == k, slot, cols)
    dest_ref[...] = cols.T[:TOP_K, :].astype(jnp.int32)


def _slots(ei, first_row, tm):
    n, wc = ei.shape
    tm = min(tm, n)
    return pl.pallas_call(
        _slot_kernel,
        grid=(n // tm,),
        in_specs=[
            pl.BlockSpec((tm, wc), lambda i: (i, 0)),
            pl.BlockSpec((1, wc), lambda i: (0, 0)),
        ],
        out_specs=pl.BlockSpec((TOP_K, tm), lambda i: (0, i)),
        out_shape=jax.ShapeDtypeStruct((TOP_K, n), jnp.int32),
        compiler_params=_params("parallel"),
        name="moe_slots",
    )(ei, first_row)


def _dispatch_kernel(dest_ref, last_ref, nv_ref, hn_ref, x_ref, zero_ref, zsem, tsem, rsem,
                     *, n_experts, n_blocks, tb, nw):
    i = pl.program_id(0)
    tm = hn_ref.shape[0] // nw
    n_tokens = pl.num_programs(0) * tm

    def row(ref, r):
        return ref.at[pl.ds(pl.multiple_of(r * nw, nw), nw), :]

    def zero_copy(block, sem):
        line0 = pl.multiple_of(jnp.maximum(block, 0) * (tb * nw), tb * nw)
        return pltpu.make_async_copy(zero_ref, x_ref.at[pl.ds(line0, tb * nw), :], sem)

    def for_expert_tails(fn):
        def body(e, c):
            @pl.when(last_ref[e] >= 0)
            def _():
                fn(zero_copy(last_ref[e], zsem.at[e]))
            return c
        lax.fori_loop(0, n_experts, body, 0)

    def for_unused_blocks(fn):
        def body(b, c):
            @pl.when(b >= nv_ref[0])
            def _():
                fn(zero_copy(b, tsem))
            return c
        lax.fori_loop(0, n_blocks, body, 0)

    @pl.when(i == 0)
    def _():
        zero_ref[...] = jnp.zeros_like(zero_ref)
        for_expert_tails(lambda cp: cp.start())
        for_unused_blocks(lambda cp: cp.start())
        for_expert_tails(lambda cp: cp.wait())
        for_unused_blocks(lambda cp: cp.wait())

    def start_one(j, c):
        for k in range(TOP_K):
            dst = dest_ref[k * n_tokens + i * tm + j]
            pltpu.make_async_copy(row(hn_ref, j), row(x_ref, dst), rsem).start()
        return c

    def wait_one(j, c):
        pltpu.make_async_copy(row(hn_ref, 0), row(x_ref, 0), rsem).wait()
        return c

    lax.fori_loop(0, tm, start_one, 0, unroll=4)
    lax.fori_loop(0, TOP_K * tm, wait_one, 0, unroll=8)


def _dispatch(hn_rows, dest, last_block, n_valid, n_blocks, nw, tm):
    n = hn_rows.shape[0] // nw
    tm = min(tm, n)
    n_experts = last_block.shape[0]
    tb = DISPATCH_BLOCK
    kern = functools.partial(_dispatch_kernel, n_experts=n_experts, n_blocks=n_blocks, tb=tb, nw=nw)
    grid_spec = pltpu.PrefetchScalarGridSpec(
        num_scalar_prefetch=3,
        grid=(n // tm,),
        in_specs=[pl.BlockSpec((tm * nw, LANES), lambda i, dest, last, nv: (i, 0))],
        out_specs=pl.BlockSpec(memory_space=pl.ANY),
        scratch_shapes=[
            pltpu.VMEM((tb * nw, LANES), jnp.uint32),
            pltpu.SemaphoreType.DMA((n_experts,)),
            pltpu.SemaphoreType.DMA,
            pltpu.SemaphoreType.DMA,
        ],
    )
    return pl.pallas_call(
        kern,
        grid_spec=grid_spec,
        out_shape=jax.ShapeDtypeStruct((n_blocks * tb * nw, LANES), jnp.uint32),
        compiler_params=_params("arbitrary"),
        name="moe_dispatch",
    )(dest, last_block, n_valid, hn_rows)


def _expert_kernel(*refs):
    la = WEIGHT_LOOKAHEAD
    be_ref, nv_ref, seg_ref = refs[:3]
    ahead = refs[3:3 + la]
    (x_ref, wg_hbm, wu_hbm, wd_hbm, o_ref, wg_buf, wu_buf, wd_buf, wgb_ref, wub_ref, wdb_ref, sem) = refs[3 + la:]
    i = pl.program_id(0)
    valid = i < nv_ref[0]
    new_expert = jnp.logical_or(i == 0, be_ref[i] != be_ref[jnp.maximum(i - 1, 0)])
    slot = seg_ref[i] % (la + 1)

    def weight_copies(e, s):
        return (pltpu.make_async_copy(wg_hbm.at[e], wg_buf.at[s], sem.at[s, 0]),
                pltpu.make_async_copy(wu_hbm.at[e], wu_buf.at[s], sem.at[s, 1]),
                pltpu.make_async_copy(wd_hbm.at[e], wd_buf.at[s], sem.at[s, 2]))

    def start_if_any(e, s):
        @pl.when(e >= 0)
        def _():
            for cp in weight_copies(e, s):
                cp.start()

    @pl.when(i == 0)
    def _():
        start_if_any(be_ref[0], 0)
        for k in range(la - 1):
            start_if_any(ahead[k][0], k + 1)

    @pl.when(jnp.logical_and(valid, new_expert))
    def _():
        start_if_any(ahead[la - 1][i], (seg_ref[i] + la) % (la + 1))
        for cp in weight_copies(be_ref[i], slot):
            cp.wait()
        wgb_ref[...] = wg_buf[slot].astype(BF16)
        wub_ref[...] = wu_buf[slot].astype(BF16)
        wdb_ref[...] = wd_buf[slot].astype(BF16)

    @pl.when(valid)
    def _():
        half = wgb_ref.shape[0] // 2
        nw = half // LANES
        x_lo, x_hi = _unpack_halves(_load_rows(x_ref, 0, x_ref.shape[0] // nw, nw))
        x_lo, x_hi = x_lo.astype(BF16), x_hi.astype(BF16)

        def proj(w_ref):
            return (jnp.dot(x_lo, w_ref[:half, :], preferred_element_type=F32)
                    + jnp.dot(x_hi, w_ref[half:, :], preferred_element_type=F32))

        hid = jax.nn.silu(proj(wgb_ref)) * proj(wub_ref)
        _store_rows(o_ref, _pack_halves(jnp.dot(hid.astype(BF16), wdb_ref[...], preferred_element_type=F32)))

    @pl.when(jnp.logical_not(valid))
    def _():
        o_ref[...] = jnp.zeros_like(o_ref)


def _experts(x_disp, block_expert, n_valid, block_segment, experts_ahead, w_gate, w_up, w_down):
    d, de = w_gate.shape[1], w_gate.shape[2]
    nw = d // (2 * LANES)
    rows = x_disp.shape[0] // nw
    tb = DISPATCH_BLOCK
    slots = WEIGHT_LOOKAHEAD + 1
    grid_spec = pltpu.PrefetchScalarGridSpec(
        num_scalar_prefetch=3 + WEIGHT_LOOKAHEAD,
        grid=(rows // tb,),
        in_specs=[
            pl.BlockSpec((tb * nw, LANES),
                         lambda i, be, nv, *_: (jnp.maximum(jnp.minimum(i, nv[0] - 1), 0), 0)),
            pl.BlockSpec(memory_space=pl.ANY),
            pl.BlockSpec(memory_space=pl.ANY),
            pl.BlockSpec(memory_space=pl.ANY),
        ],
        out_specs=pl.BlockSpec((tb * nw, LANES), lambda i, *_: (i, 0)),
        scratch_shapes=[
            pltpu.VMEM((slots, d, de), w_gate.dtype),
            pltpu.VMEM((slots, d, de), w_up.dtype),
            pltpu.VMEM((slots, de, d), w_down.dtype),
            pltpu.VMEM((d, de), BF16),
            pltpu.VMEM((d, de), BF16),
            pltpu.VMEM((de, d), BF16),
            pltpu.SemaphoreType.DMA((slots, 3)),
        ],
    )
    return pl.pallas_call(
        _expert_kernel,
        grid_spec=grid_spec,
        out_shape=jax.ShapeDtypeStruct((rows * nw, LANES), jnp.uint32),
        compiler_params=_params("arbitrary"),
        name="moe_experts",
    )(block_expert, n_valid, block_segment, *experts_ahead, x_disp, w_gate, w_up, w_down)


def _combined_halves(dest_ref, h_ref, ew_ref, y_ref, buf_ref, sem):
    tm, d = h_ref.shape
    nw = d // (2 * LANES)
    i = pl.program_id(0)
    n_tiles = pl.num_programs(0)

    def row(ref, r):
        return ref.at[pl.ds(pl.multiple_of(r * nw, nw), nw), :]

    def start_tile(tile, slot):
        def start_one(j, c):
            for k in range(TOP_K):
                src = dest_ref[k * (n_tiles * tm) + tile * tm + j]
                pltpu.make_async_copy(row(y_ref, src), row(buf_ref.at[slot], k * tm + j), sem.at[slot]).start()
            return c

        lax.fori_loop(0, tm, start_one, 0, unroll=4)

    def wait_tile(slot):
        def wait_one(j, c):
            pltpu.make_async_copy(row(y_ref, 0), row(buf_ref.at[slot], 0), sem.at[slot]).wait()
            return c

        lax.fori_loop(0, TOP_K * tm, wait_one, 0, unroll=8)

    slot = i % 2

    @pl.when(i == 0)
    def _():
        start_tile(0, 0)

    @pl.when(i + 1 < n_tiles)
    def _():
        start_tile(i + 1, 1 - slot)

    wait_tile(slot)
    h = h_ref[...]
    w = ew_ref[...]
    lo, hi = h[:, :d // 2], h[:, d // 2:]
    for k in range(TOP_K):
        y_lo, y_hi = _unpack_halves(_load_rows(buf_ref.at[slot], k * tm, tm, nw))
        lo = lo + y_lo * w[:, k:k + 1]
        hi = hi + y_hi * w[:, k:k + 1]
    return lo, hi


def _combine_norm_kernel(dest_ref, h_ref, ew_ref, g_ref, y_ref, o_ref, buf_ref, sem):
    d = h_ref.shape[1]
    lo, hi = _combined_halves(dest_ref, h_ref, ew_ref, y_ref, buf_ref, sem)
    ms = (jnp.sum(lo * lo, axis=-1, keepdims=True) + jnp.sum(hi * hi, axis=-1, keepdims=True)) / d
    scale = lax.rsqrt(ms + RMS_EPS)
    o_ref[:, :d // 2] = lo * scale * g_ref[:, :d // 2]
    o_ref[:, d // 2:] = hi * scale * g_ref[:, d // 2:]


def _combine_norm(h2, y_disp, dest, ew, gain, tm):
    n, d = h2.shape
    tm = min(tm, n)
    nw = d // (2 * LANES)
    wc = ew.shape[1]
    grid_spec = pltpu.PrefetchScalarGridSpec(
        num_scalar_prefetch=1,
        grid=(n // tm,),
        in_specs=[
            pl.BlockSpec((tm, d), lambda i, dest: (i, 0)),
            pl.BlockSpec((tm, wc), lambda i, dest: (i, 0)),
            pl.BlockSpec((1, d), lambda i, dest: (0, 0)),
            pl.BlockSpec(memory_space=pl.ANY),
        ],
        out_specs=pl.BlockSpec((tm, d), lambda i, dest: (i, 0)),
        scratch_shapes=[
            pltpu.VMEM((2, TOP_K * tm * nw, LANES), jnp.uint32),
            pltpu.SemaphoreType.DMA((2,)),
        ],
    )
    return pl.pallas_call(
        _combine_norm_kernel,
        grid_spec=grid_spec,
        out_shape=jax.ShapeDtypeStruct((n, d), F32),
        compiler_params=_params("arbitrary"),
        name="moe_combine",
    )(dest, h2, ew, gain.reshape(1, d), y_disp)


def _combine_proj_kernel(dest_ref, h_ref, ew_ref, g_ref, y_ref, w_ref, cs_ref, ws_ref,
                         o_ref, side_ref, hnew_ref, xn_ref, buf_ref, sem):
    d = h_ref.shape[1]
    half = d // 2

    @pl.when(pl.program_id(1) == 0)
    def _():
        lo, hi = _combined_halves(dest_ref, h_ref, ew_ref, y_ref, buf_ref, sem)
        hnew_ref[:, :half] = lo
        hnew_ref[:, half:] = hi
        ms = (jnp.sum(lo * lo, axis=-1, keepdims=True) + jnp.sum(hi * hi, axis=-1, keepdims=True)) / d
        scale = lax.rsqrt(ms + RMS_EPS)
        xn_ref[:, :half] = (lo * scale * g_ref[:, :half]).astype(BF16)
        xn_ref[:, half:] = (hi * scale * g_ref[:, half:]).astype(BF16)
        side_ref[...] = jnp.dot(xn_ref[...], ws_ref[...], preferred_element_type=F32)

    acc = jnp.dot(xn_ref[...], w_ref[...], preferred_element_type=F32)
    o_ref[...] = (acc * cs_ref[...]).astype(o_ref.dtype)


def _combine_proj(h2, y_disp, dest, ew, gain, w, n_cols, col_scale, w_side, out_dtype, ts, tn):
    n, d = h2.shape
    ts, tn = min(ts, n), min(tn, n_cols)
    nw = d // (2 * LANES)
    wc = ew.shape[1]
    ms = w_side.shape[1]
    grid_spec = pltpu.PrefetchScalarGridSpec(
        num_scalar_prefetch=1,
        grid=(n // ts, n_cols // tn),
        in_specs=[
            pl.BlockSpec((ts, d), lambda i, j, dest: (i, 0)),
            pl.BlockSpec((ts, wc), lambda i, j, dest: (i, 0)),
            pl.BlockSpec((1, d), lambda i, j, dest: (0, 0)),
            pl.BlockSpec(memory_space=pl.ANY),
            pl.BlockSpec((d, tn), lambda i, j, dest: (0, j)),
            pl.BlockSpec((1, tn), lambda i, j, dest: (0, j)),
            pl.BlockSpec((d, ms), lambda i, j, dest: (0, 0)),
        ],
        out_specs=[
            pl.BlockSpec((ts, tn), lambda i, j, dest: (i, j)),
            pl.BlockSpec((ts, ms), lambda i, j, dest: (i, 0)),
            pl.BlockSpec((ts, d), lambda i, j, dest: (i, 0)),
        ],
        scratch_shapes=[
            pltpu.VMEM((ts, d), BF16),
            pltpu.VMEM((2, TOP_K * ts * nw, LANES), jnp.uint32),
            pltpu.SemaphoreType.DMA((2,)),
        ],
    )
    return pl.pallas_call(
        _combine_proj_kernel,
        grid_spec=grid_spec,
        out_shape=[
            jax.ShapeDtypeStruct((n, n_cols), out_dtype),
            jax.ShapeDtypeStruct((n, ms), F32),
            jax.ShapeDtypeStruct((n, d), F32),
        ],
        compiler_params=_params("arbitrary", "arbitrary"),
        name="combine_proj",
    )(dest, h2, ew, gain.reshape(1, d), y_disp, w, col_scale.reshape(1, n_cols), w_side)


def _moe_experts_out(h2, norm_gain, w_group, b_group, w_expert, b_expert, w_gate, w_up, w_down):
    n, d = h2.shape
    n_groups = w_group.shape[1]
    n_experts = w_expert.shape[1]
    epg = n_experts // n_groups
    n_logits = n_groups + n_experts
    wc = -(-n_logits // LANES) * LANES
    w_router = jnp.concatenate(
        [w_group.astype(F32), w_expert.astype(F32), jnp.zeros((d, wc - n_logits), F32)], axis=1)
    b_router = jnp.concatenate(
        [b_group.astype(F32), b_expert.astype(F32), jnp.zeros((wc - n_logits,), F32)]).reshape(1, wc)
    hn_packed, ei, ew, counts = _router(h2, norm_gain, w_router, b_router, n_groups, epg, tm=512)

    tb = DISPATCH_BLOCK
    n_pairs = n * TOP_K
    counts = counts[0, :n_experts]
    blocks_per_expert = (counts + tb - 1) // tb
    block_end = jnp.cumsum(blocks_per_expert)
    block_start = block_end - blocks_per_expert
    first_row = jnp.pad((block_start * tb).astype(F32), (0, wc - n_experts)).reshape(1, wc)
    dest = _slots(ei, first_row, tm=512).reshape(-1)
    last_block = jnp.where(blocks_per_expert > 0, block_end - 1, -1).astype(jnp.int32)
    n_blocks = -(-n_pairs // tb) + n_experts
    n_valid = block_end[-1:].astype(jnp.int32)
    blk = jnp.minimum(jnp.arange(n_blocks, dtype=jnp.int32), n_valid[0] - 1)
    block_expert = jnp.searchsorted(block_end, blk, side="right").astype(jnp.int32)
    expert_range = jnp.arange(n_experts, dtype=jnp.int32)
    owners = jnp.where(blocks_per_expert > 0, expert_range, n_experts)
    owner_at_or_after = lax.cummin(owners, axis=0, reverse=True)
    owner_after = jnp.concatenate([owner_at_or_after[1:], jnp.full((2,), n_experts, jnp.int32)])
    experts_ahead = []
    step = block_expert
    for _ in range(WEIGHT_LOOKAHEAD):
        step = jnp.take(owner_after, step)
        experts_ahead.append(jnp.where(step < n_experts, step, -1).astype(jnp.int32))
    changed = jnp.concatenate([jnp.zeros((1,), jnp.int32),
                               (block_expert[1:] != block_expert[:-1]).astype(jnp.int32)])
    block_segment = jnp.cumsum(changed).astype(jnp.int32)

    x_disp = _dispatch(hn_packed, dest, last_block, n_valid, n_blocks, d // (2 * LANES), tm=512)
    y_disp = _experts(x_disp, block_expert, n_valid, block_segment, experts_ahead, w_gate, w_up, w_down)
    return y_disp, dest, ew


def _forget_kernel(f_ref, b_ref, c_ref):
    heads = c_ref.shape[0]
    x = jax.nn.log_sigmoid(f_ref[...].T[:heads, :] + b_ref[...])
    s = x.shape[1]
    lane = lax.broadcasted_iota(jnp.int32, x.shape, 1)
    shift = 1
    while shift < s:
        x = x + jnp.where(lane >= shift, pltpu.roll(x, shift, axis=1), 0.0)
        shift *= 2
    c_ref[...] = x


def _forget_cumsum(f_logit, b_forget, b, s):
    h = b_forget.shape[0]
    return pl.pallas_call(
        _forget_kernel,
        grid=(b,),
        in_specs=[
            pl.BlockSpec((s, f_logit.shape[1]), lambda i: (i, 0)),
            pl.BlockSpec((h, 1), lambda i: (0, 0)),
        ],
        out_specs=pl.BlockSpec((None, h, s), lambda i: (i, 0, 0)),
        out_shape=jax.ShapeDtypeStruct((b, h, s), F32),
        compiler_params=_params("parallel"),
        name="fox_forget",
    )(f_logit, b_forget.astype(F32).reshape(h, 1))


def _fox_attn_kernel(q_ref, k_ref, v_ref, c_ref, o_ref, *, tk, n_sub):
    seq, dh = q_ref.shape
    head = pl.program_id(1)
    log2e = math.log2(math.e)
    row = lax.broadcasted_iota(jnp.int32, (tk, tk), 0)
    col = lax.broadcasted_iota(jnp.int32, (tk, tk), 1)
    causal = col <= row

    def attend(q, k, v, bias, carry, mask):
        m, l, acc = carry
        s = lax.dot_general(q, k, (((1,), (1,)), ((), ())), preferred_element_type=F32) + bias
        if mask:
            s = jnp.where(causal, s, -jnp.inf)
        m_new = jnp.maximum(m, jnp.max(s, axis=-1, keepdims=True))
        alpha = jnp.exp2(m - m_new)
        p = jnp.exp2(s - m_new)
        l = alpha * l + jnp.sum(p, axis=-1, keepdims=True)
        acc = alpha * acc + jnp.dot(p.astype(BF16), v, preferred_element_type=F32)
        return m_new, l, acc

    def key_block(k0):
        bias = c_ref[pl.ds(head, 1), pl.ds(k0, tk)] * (-log2e)
        return k_ref[pl.ds(k0, tk), :], v_ref[pl.ds(k0, tk), :], bias

    for qt in range(seq // (tk * n_sub)):
        first = qt * n_sub
        qs = [q_ref[(first + a) * tk:(first + a + 1) * tk, :] for a in range(n_sub)]
        init = (jnp.full((tk, 1), -jnp.inf, F32), jnp.zeros((tk, 1), F32), jnp.zeros((tk, dh), F32))
        carries = (init,) * n_sub

        def kv_step(ki, carries, qs=qs):
            k, v, bias = key_block(pl.multiple_of(ki * tk, tk))
            return tuple(attend(qs[a], k, v, bias, carries[a], False) for a in range(n_sub))

        if qt > 0:
            carries = lax.fori_loop(0, first, kv_step, carries)
        carries = list(carries)
        for kb in range(n_sub):
            k, v, bias = key_block((first + kb) * tk)
            for a in range(kb, n_sub):
                carries[a] = attend(qs[a], k, v, bias, carries[a], a == kb)
        for a in range(n_sub):
            _, l, acc = carries[a]
            o_ref[(first + a) * tk:(first + a + 1) * tk, :] = (acc / l).astype(o_ref.dtype)


def _fox_attention(proj, cum, batch, seq, heads, tk, n_sub):
    dh = FOX_HEAD_DIM
    tk = min(tk, seq)
    n_sub = min(n_sub, seq // tk)
    kern = functools.partial(_fox_attn_kernel, tk=tk, n_sub=n_sub)
    return pl.pallas_call(
        kern,
        grid=(batch, heads),
        in_specs=[
            pl.BlockSpec((seq, dh), lambda b, h: (b, h)),
            pl.BlockSpec((seq, dh), lambda b, h: (b, heads + h)),
            pl.BlockSpec((seq, dh), lambda b, h: (b, 2 * heads + h)),
            pl.BlockSpec((None, heads, seq), lambda b, h: (b, 0, 0)),
        ],
        out_specs=pl.BlockSpec((seq, dh), lambda b, h: (b, h)),
        out_shape=jax.ShapeDtypeStruct((batch * seq, heads * dh), BF16),
        compiler_params=_params("parallel", "parallel"),
        name="fox_attention",
    )(proj, proj, proj, cum)


def _fox_out_kernel(o_ref, gate_ref, w_ref, h_ref, out_ref):
    gated = (o_ref[...].astype(F32) * jax.nn.sigmoid(gate_ref[...].astype(F32))).astype(BF16)
    out_ref[...] = h_ref[...] + jnp.dot(gated, w_ref[...], preferred_element_type=F32)


def _fox_out(o, proj, w_out, h2, tm):
    n, d = h2.shape
    tm = min(tm, n)
    gate_block = 3
    return pl.pallas_call(
        _fox_out_kernel,
        grid=(n // tm,),
        in_specs=[
            pl.BlockSpec((tm, d), lambda i: (i, 0)),
            pl.BlockSpec((tm, d), lambda i: (i, gate_block)),
            pl.BlockSpec((d, d), lambda i: (0, 0)),
            pl.BlockSpec((tm, d), lambda i: (i, 0)),
        ],
        out_specs=pl.BlockSpec((tm, d), lambda i: (i, 0)),
        out_shape=jax.ShapeDtypeStruct((n, d), F32),
        compiler_params=_params("parallel"),
        name="fox_out",
    )(o, proj, w_out, h2)


def _s5_layer(x, norm_gain, w_in, lam_re, lam_im, b_re, b_im, c_re, c_im, d_skip, log_step, w_glu, b_glu, w_out):
    b, s, d = x.shape
    assert b == STATE_BATCH
    ds = w_in.shape[1]
    bmat, cmat, a_re, a_im = _s5_discretise(lam_re, lam_im, b_re, b_im, c_re, c_im, log_step)
    u_tm = _s5_in(x, norm_gain, w_in.astype(BF16), ts=64)
    y_tm = _s5_scan(u_tm, bmat, cmat, a_re, a_im, d_skip.astype(F32), t_chunk=128)
    return _s5_out(y_tm, x, w_glu.astype(BF16), b_glu.astype(F32), w_out.astype(BF16), ts=64)


def _fox_layer(h2, moe_rows, moe_dest, moe_w, b, s, norm_gain, w_in, b_forget, w_out):
    d = h2.shape[1]
    heads = d // FOX_HEAD_DIM
    w_all = w_in.astype(BF16)
    w_forget = jnp.pad(w_in[:, 4 * d:], ((0, 0), (0, LANES - heads))).astype(BF16)
    q_scale = FOX_HEAD_DIM ** -0.5 * math.log2(math.e)
    col_scale = jnp.concatenate([jnp.full((d,), q_scale, F32), jnp.ones((3 * d,), F32)])
    proj, f_logit, h_new = _combine_proj(h2, moe_rows, moe_dest, moe_w, norm_gain, w_all, 4 * d, col_scale,
                                         w_forget, BF16, ts=512, tn=1024)
    cum = _forget_cumsum(f_logit, b_forget, b, s)
    o = _fox_attention(proj, cum, b, s, heads, tk=512, n_sub=4)
    return _fox_out(o, proj, w_out.astype(BF16), h_new, tm=512)


def kernel(x, l0_mix_norm, l0_s5_w_in, l0_s5_lambda_re, l0_s5_lambda_im, l0_s5_b_re, l0_s5_b_im, l0_s5_c_re, l0_s5_c_im, l0_s5_d, l0_s5_log_step, l0_s5_w_glu, l0_s5_b_glu, l0_s5_w_out, l0_ffn_norm, l0_moe_w_group, l0_moe_b_group, l0_moe_w_expert, l0_moe_b_expert, l0_moe_w_gate, l0_moe_w_up, l0_moe_w_down, l1_mix_norm, l1_fox_w_in, l1_fox_b_forget, l1_fox_w_out, l1_ffn_norm, l1_moe_w_group, l1_moe_b_group, l1_moe_w_expert, l1_moe_b_expert, l1_moe_w_gate, l1_moe_w_up, l1_moe_w_down, final_norm):
    b, s, d = x.shape
    h = _s5_layer(x, l0_mix_norm, l0_s5_w_in, l0_s5_lambda_re, l0_s5_lambda_im, l0_s5_b_re, l0_s5_b_im,
                  l0_s5_c_re, l0_s5_c_im, l0_s5_d, l0_s5_log_step, l0_s5_w_glu, l0_s5_b_glu, l0_s5_w_out)
    h2 = h.reshape(b * s, d)
    rows, dest, pair_w = _moe_experts_out(h2, l0_ffn_norm, l0_moe_w_group, l0_moe_b_group, l0_moe_w_expert,
                                          l0_moe_b_expert, l0_moe_w_gate, l0_moe_w_up, l0_moe_w_down)
    h2 = _fox_layer(h2, rows, dest, pair_w, b, s, l1_mix_norm, l1_fox_w_in, l1_fox_b_forget, l1_fox_w_out)
    rows, dest, pair_w = _moe_experts_out(h2, l1_ffn_norm, l1_moe_w_group, l1_moe_b_group, l1_moe_w_expert,
                                          l1_moe_b_expert, l1_moe_w_gate, l1_moe_w_up, l1_moe_w_down)
    return _combine_norm(h2, rows, dest, pair_w, final_norm, tm=512).reshape(b, s, d)
```

```python
import functools
import math

import jax
import jax.numpy as jnp
from jax import lax
from jax.experimental import pallas as pl
from jax.experimental.pallas import tpu as pltpu

F32 = jnp.float32
BF16 = jnp.bfloat16

RMS_EPS = 1e-6
STATE_BATCH = 8
SSM_GROUPS_PER_BLOCK = 8
FOX_HEAD_DIM = 128
TOP_K = 2
DISPATCH_BLOCK = 256
WEIGHT_LOOKAHEAD = 1
LANES = 128
VMEM_LIMIT = 56 * 1024 * 1024


def _params(*sem):
    return pltpu.CompilerParams(dimension_semantics=sem, vmem_limit_bytes=VMEM_LIMIT)


def _rms(x, gain):
    ms = jnp.mean(x * x, axis=-1, keepdims=True)
    return x * lax.rsqrt(ms + RMS_EPS) * gain


def _rms_matmul_kernel(x_ref, g_ref, w_ref, cs_ref, ws_ref, o_ref, side_ref, xn_ref):
    @pl.when(pl.program_id(2) == 0)
    def _():
        xn_ref[...] = _rms(x_ref[...], g_ref[...]).astype(BF16)
        side_ref[...] = jnp.dot(xn_ref[...], ws_ref[...], preferred_element_type=F32)

    acc = jnp.dot(xn_ref[...], w_ref[...], preferred_element_type=F32)
    o_ref[...] = (acc * cs_ref[...]).astype(o_ref.dtype)


def _rms_matmul(x, gain, w, n_cols, col_scale, w_side, out_dtype, ts, tn):
    b, s, d = x.shape
    m = n_cols
    ms = w_side.shape[1]
    ts, tn = min(ts, s), min(tn, m)
    ns = s // ts
    return pl.pallas_call(
        _rms_matmul_kernel,
        grid=(b, ns, m // tn),
        in_specs=[
            pl.BlockSpec((None, ts, d), lambda bi, si, j: (bi, si, 0)),
            pl.BlockSpec((1, d), lambda bi, si, j: (0, 0)),
            pl.BlockSpec((d, tn), lambda bi, si, j: (0, j)),
            pl.BlockSpec((1, tn), lambda bi, si, j: (0, j)),
            pl.BlockSpec((d, ms), lambda bi, si, j: (0, 0)),
        ],
        out_specs=[
            pl.BlockSpec((ts, tn), lambda bi, si, j: (bi * ns + si, j)),
            pl.BlockSpec((ts, ms), lambda bi, si, j: (bi * ns + si, 0)),
        ],
        out_shape=[jax.ShapeDtypeStruct((b * s, m), out_dtype), jax.ShapeDtypeStruct((b * s, ms), F32)],
        scratch_shapes=[pltpu.VMEM((ts, d), BF16)],
        compiler_params=_params("parallel", "parallel", "arbitrary"),
        name="rms_matmul",
    )(x, gain.reshape(1, d), w, col_scale.reshape(1, m), w_side)


def _s5_in_kernel(x_ref, g_ref, w_ref, u_ref):
    nb, ts, d = x_ref.shape
    nl = w_ref.shape[1] // LANES
    xn = _rms(x_ref[...].reshape(nb * ts, d), g_ref[...]).astype(BF16)
    u = jnp.dot(xn, w_ref[...], preferred_element_type=F32)
    for bi in range(nb):
        for c in range(nl):
            u_ref[pl.ds(bi * nl + c, ts, stride=nb * nl), :] = u[bi * ts:(bi + 1) * ts, c * LANES:(c + 1) * LANES]


def _s5_in(x, gain, w, ts):
    b, s, d = x.shape
    m = w.shape[1]
    nl = m // LANES
    ts = min(ts, s)
    return pl.pallas_call(
        _s5_in_kernel,
        grid=(s // ts,),
        in_specs=[
            pl.BlockSpec((b, ts, d), lambda si: (0, si, 0)),
            pl.BlockSpec((1, d), lambda si: (0, 0)),
            pl.BlockSpec((d, m), lambda si: (0, 0)),
        ],
        out_specs=pl.BlockSpec((ts * b * nl, LANES), lambda si: (si, 0)),
        out_shape=jax.ShapeDtypeStruct((s * b * nl, LANES), F32),
        compiler_params=_params("parallel"),
        name="s5_in",
    )(x, gain.reshape(1, d), w)


def _s5_scan_kernel(u_ref, bmat_ref, cmat_ref, are_ref, aim_ref, d_ref, y_ref, bu2_ref, st_ref, *, n_blocks, t_chunk):
    @pl.when(pl.program_id(0) == 0)
    def _():
        st_ref[...] = jnp.zeros_like(st_ref)

    cw = bmat_ref.shape[1]
    sw = bmat_ref.shape[2] // 2
    rows = t_chunk * STATE_BATCH
    for gb in range(n_blocks):
        u_blk = u_ref[pl.ds(gb, rows, stride=n_blocks), :]
        bu_ref = bu2_ref.at[gb % 2]
        bu_ref[...] = jnp.dot(u_blk.astype(BF16), bmat_ref[gb], preferred_element_type=F32)
        a_re = jnp.broadcast_to(are_ref[gb], (STATE_BATCH, sw))
        a_im = jnp.broadcast_to(aim_ref[gb], (STATE_BATCH, sw))

        def step(t, carry):
            s_re, s_im = carry
            r0 = pl.multiple_of(t * STATE_BATCH, STATE_BATCH)
            b_re = bu_ref[pl.ds(r0, STATE_BATCH), 0:sw]
            b_im = bu_ref[pl.ds(r0, STATE_BATCH), sw:2 * sw]
            n_re = a_re * s_re - a_im * s_im + b_re
            n_im = a_re * s_im + a_im * s_re + b_im
            bu_ref[pl.ds(r0, STATE_BATCH), 0:sw] = n_re
            bu_ref[pl.ds(r0, STATE_BATCH), sw:2 * sw] = n_im
            return n_re, n_im

        s_re, s_im = lax.fori_loop(0, t_chunk, step, (st_ref[gb, 0], st_ref[gb, 1]), unroll=True)
        st_ref[gb, 0] = s_re
        st_ref[gb, 1] = s_im
        y = jnp.dot(bu_ref[...].astype(BF16), cmat_ref[gb], preferred_element_type=F32)
        y_ref[pl.ds(gb, rows, stride=n_blocks), :] = y + d_ref[:, gb * cw:(gb + 1) * cw] * u_blk


def _s5_scan(u_tm, bmat, cmat, a_re, a_im, d_skip, t_chunk):
    nb, cw, sw2 = bmat.shape
    assert cw == LANES
    ds = nb * cw
    seq = u_tm.shape[0] // (STATE_BATCH * nb)
    t_chunk = min(t_chunk, seq)
    tr = t_chunk * STATE_BATCH
    kern = functools.partial(_s5_scan_kernel, n_blocks=nb, t_chunk=t_chunk)
    return pl.pallas_call(
        kern,
        grid=(seq // t_chunk,),
        in_specs=[
            pl.BlockSpec((tr * nb, LANES), lambda i: (i, 0)),
            pl.BlockSpec((nb, cw, sw2), lambda i: (0, 0, 0)),
            pl.BlockSpec((nb, sw2, cw), lambda i: (0, 0, 0)),
            pl.BlockSpec((nb, 1, sw2 // 2), lambda i: (0, 0, 0)),
            pl.BlockSpec((nb, 1, sw2 // 2), lambda i: (0, 0, 0)),
            pl.BlockSpec((1, ds), lambda i: (0, 0)),
        ],
        out_specs=pl.BlockSpec((tr * nb, LANES), lambda i: (i, 0)),
        out_shape=jax.ShapeDtypeStruct(u_tm.shape, F32),
        scratch_shapes=[
            pltpu.VMEM((2, tr, sw2), F32),
            pltpu.VMEM((nb, 2, STATE_BATCH, sw2 // 2), F32),
        ],
        compiler_params=_params("arbitrary"),
        name="s5_scan",
    )(u_tm, bmat, cmat, a_re, a_im, d_skip.reshape(1, ds))


def _s5_discretise(lam_re, lam_im, b_re, b_im, c_re, c_im, log_step):
    lr, li = lam_re.astype(F32), lam_im.astype(F32)
    step = jnp.exp(log_step.astype(F32))[:, None]
    mag = jnp.exp(lr * step)
    ab_re = mag * jnp.cos(li * step)
    ab_im = mag * jnp.sin(li * step)
    den = lr * lr + li * li
    nr, ni = ab_re - 1.0, ab_im
    fr = (nr * lr + ni * li) / den
    fi = (ni * lr - nr * li) / den
    br, bi = b_re.astype(F32), b_im.astype(F32)
    bb_re = fr[..., None] * br - fi[..., None] * bi
    bb_im = fr[..., None] * bi + fi[..., None] * br
    g, n, c = bb_re.shape
    gpb = min(SSM_GROUPS_PER_BLOCK, g)
    nb = g // gpb
    eye = jnp.eye(gpb, dtype=F32)

    def in_blocks(bb):
        return jnp.einsum("bjnc,jk->bjckn", bb.reshape(nb, gpb, n, c), eye).reshape(nb, gpb * c, gpb * n)

    def out_blocks(cc):
        return jnp.einsum("bjcn,jk->bjnkc", cc.reshape(nb, gpb, c, n), eye).reshape(nb, gpb * n, gpb * c)

    bmat = jnp.concatenate([in_blocks(bb_re), in_blocks(bb_im)], axis=-1).astype(BF16)
    cmat = jnp.concatenate([out_blocks(c_re.astype(F32)), -out_blocks(c_im.astype(F32))], axis=1).astype(BF16)
    return bmat, cmat, ab_re.reshape(nb, 1, gpb * n), ab_im.reshape(nb, 1, gpb * n)


def _s5_out_kernel(y_ref, x_ref, wg_ref, bg_ref, wo_ref, o_ref):
    nb, ts, d = x_ref.shape
    nl = wg_ref.shape[0] // LANES
    y = jnp.concatenate(
        [jnp.concatenate([y_ref[pl.ds(bi * nl + c, ts, stride=nb * nl), :] for c in range(nl)], axis=1)
         for bi in range(nb)], axis=0)
    g = jax.nn.gelu(y)
    z = jnp.dot(g.astype(BF16), wg_ref[...], preferred_element_type=F32) + bg_ref[...]
    gated = g * jax.nn.sigmoid(z)
    out = jnp.dot(gated.astype(BF16), wo_ref[...], preferred_element_type=F32)
    o_ref[...] = x_ref[...] + out.reshape(nb, ts, d)


def _s5_out(y_tm, x, w_glu, b_glu, w_out, ts):
    b, s, d = x.shape
    ds = w_glu.shape[0]
    nl = ds // LANES
    ts = min(ts, s)
    return pl.pallas_call(
        _s5_out_kernel,
        grid=(s // ts,),
        in_specs=[
            pl.BlockSpec((ts * b * nl, LANES), lambda si: (si, 0)),
            pl.BlockSpec((b, ts, d), lambda si: (0, si, 0)),
            pl.BlockSpec((ds, ds), lambda si: (0, 0)),
            pl.BlockSpec((1, ds), lambda si: (0, 0)),
            pl.BlockSpec((ds, d), lambda si: (0, 0)),
        ],
        out_specs=pl.BlockSpec((b, ts, d), lambda si: (0, si, 0)),
        out_shape=jax.ShapeDtypeStruct((b, s, d), F32),
        compiler_params=_params("parallel"),
        name="s5_out",
    )(y_tm, x, w_glu, b_glu.reshape(1, ds), w_out)


def _pack_halves(x):
    half = x.shape[-1] // 2
    lo = lax.bitcast_convert_type(x[:, :half].astype(BF16).astype(F32), jnp.uint32)
    hi = lax.bitcast_convert_type(x[:, half:].astype(BF16).astype(F32), jnp.uint32)
    return (lo >> 16) | (hi & jnp.uint32(0xFFFF0000))


def _unpack_halves(words):
    lo = lax.bitcast_convert_type(words << 16, F32)
    hi = lax.bitcast_convert_type(words & jnp.uint32(0xFFFF0000), F32)
    return lo, hi


def _store_rows(ref, words):
    rows, nw = words.shape[0], words.shape[1] // LANES
    for c in range(nw):
        ref[pl.ds(c, rows, stride=nw), :] = words[:, c * LANES:(c + 1) * LANES]


def _load_rows(ref, first_row, rows, nw):
    return jnp.concatenate(
        [ref[pl.ds(first_row * nw + c, rows, stride=nw), :] for c in range(nw)], axis=1)


def _router_kernel(h_ref, g_ref, whi_ref, wlo_ref, b_ref, tri_ref, hn_ref, ei_ref, ew_ref, cnt_ref, base_ref,
                   *, n_groups, epg):
    @pl.when(pl.program_id(0) == 0)
    def _():
        base_ref[...] = jnp.zeros_like(base_ref)

    tp = tri_ref.shape[0]
    nw = hn_ref.shape[0] // h_ref.shape[0]

    def route(first):
        hn = _rms(h_ref[first:first + tp, :], g_ref[...])
        _store_rows(hn_ref.at[first * nw:(first + tp) * nw, :], _pack_halves(hn))
        hn_hi = hn.astype(BF16)
        hn_lo = (hn - hn_hi.astype(F32)).astype(BF16)
        logits = (jnp.dot(hn_hi, whi_ref[...], preferred_element_type=F32)
                  + jnp.dot(hn_lo, whi_ref[...], preferred_element_type=F32)
                  + jnp.dot(hn_hi, wlo_ref[...], preferred_element_type=F32)) + b_ref[...]
        lane = lax.broadcasted_iota(jnp.int32, logits.shape, 1)
        n_lanes = logits.shape[1]

        def first_argmax(vals, vmax):
            first_lane = jnp.min(jnp.where(vals == vmax, lane, n_lanes).astype(F32), axis=-1, keepdims=True)
            return first_lane.astype(jnp.int32)

        in_groups = lane < n_groups
        gl = jnp.where(in_groups, logits, -jnp.inf)
        g_max = jnp.max(gl, axis=-1, keepdims=True)
        g_sel = first_argmax(gl, g_max)
        p_group = 1.0 / jnp.sum(jnp.where(in_groups, jnp.exp(logits - g_max), 0.0), axis=-1, keepdims=True)

        lo = n_groups + g_sel * epg
        el = jnp.where(jnp.logical_and(lane >= lo, lane < lo + epg), logits, -jnp.inf)
        t1 = jnp.max(el, axis=-1, keepdims=True)
        i1 = first_argmax(el, t1)
        el2 = jnp.where(lane == i1, -jnp.inf, el)
        t2 = jnp.max(el2, axis=-1, keepdims=True)
        i2 = first_argmax(el2, t2)
        e2x = jnp.exp(t2 - t1)
        w1 = p_group / (1.0 + e2x)
        w2 = p_group * e2x / (1.0 + e2x)
        e1 = i1 - n_groups
        e2 = i2 - n_groups

        onehot = jnp.logical_or(lane == e1, lane == e2)
        earlier = jnp.dot(tri_ref[...], jnp.where(onehot, 1.0, 0.0).astype(BF16), preferred_element_type=F32)
        total = earlier + base_ref[...]
        r1 = jnp.sum(jnp.where(lane == e1, total, 0.0), axis=-1, keepdims=True).astype(jnp.int32)
        r2 = jnp.sum(jnp.where(lane == e2, total, 0.0), axis=-1, keepdims=True).astype(jnp.int32)
        base_ref[...] = base_ref[...] + jnp.sum(jnp.where(onehot, 1.0, 0.0), axis=0, keepdims=True)

        ei_ref[first:first + tp, :] = jnp.where(
            lane == 0, e1, jnp.where(lane == 1, e2, jnp.where(lane == 2, r1, jnp.where(lane == 3, r2, 0))))
        ew_ref[first:first + tp, :] = jnp.where(lane == 0, w1, jnp.where(lane == 1, w2, 0.0))

    for part in range(h_ref.shape[0] // tp):
        route(part * tp)
    cnt_ref[...] = base_ref[...].astype(jnp.int32)


def _router(h2, gain, w_router, b_router, n_groups, epg, tm, tp):
    n, d = h2.shape
    tm = min(tm, n)
    tp = min(tp, tm)
    wc = w_router.shape[1]
    nw = d // (2 * LANES)
    w_hi = w_router.astype(BF16)
    w_lo = (w_router - w_hi.astype(F32)).astype(BF16)
    tri = jnp.tri(tp, k=-1, dtype=BF16)
    kern = functools.partial(_router_kernel, n_groups=n_groups, epg=epg)
    return pl.pallas_call(
        kern,
        grid=(n // tm,),
        in_specs=[
            pl.BlockSpec((tm, d), lambda i: (i, 0)),
            pl.BlockSpec((1, d), lambda i: (0, 0)),
            pl.BlockSpec((d, wc), lambda i: (0, 0)),
            pl.BlockSpec((d, wc), lambda i: (0, 0)),
            pl.BlockSpec((1, wc), lambda i: (0, 0)),
            pl.BlockSpec((tp, tp), lambda i: (0, 0)),
        ],
        out_specs=[
            pl.BlockSpec((tm * nw, LANES), lambda i: (i, 0)),
            pl.BlockSpec((tm, wc), lambda i: (i, 0)),
            pl.BlockSpec((tm, wc), lambda i: (i, 0)),
            pl.BlockSpec((1, wc), lambda i: (0, 0)),
        ],
        out_shape=[
            jax.ShapeDtypeStruct((n * nw, LANES), jnp.uint32),
            jax.ShapeDtypeStruct((n, wc), jnp.int32),
            jax.ShapeDtypeStruct((n, wc), F32),
            jax.ShapeDtypeStruct((1, wc), jnp.int32),
        ],
        scratch_shapes=[pltpu.VMEM((1, wc), F32)],
        compiler_params=_params("arbitrary"),
        name="moe_router",
    )(h2, gain.reshape(1, d), w_hi, w_lo, b_router, tri)


def _slot_kernel(ei_ref, first_ref, dest_ref):
    ei = ei_ref[...]
    lane = lax.broadcasted_iota(jnp.int32, ei.shape, 1)
    cols = jnp.zeros(ei.shape, F32)
    for k in range(TOP_K):
        first = jnp.sum(jnp.where(lane == ei[:, k:k + 1], first_ref[...], 0.0), axis=-1, keepdims=True)
        slot = first + ei[:, TOP_K + k:TOP_K + k + 1].astype(F32)
        cols = jnp.where(lane == k, slot, cols)
    dest_ref[...] = cols.T[:TOP_K, :].astype(jnp.int32)


def _slots(ei, first_row, tm):
    n, wc = ei.shape
    tm = min(tm, n)
    return pl.pallas_call(
        _slot_kernel,
        grid=(n // tm,),
        in_specs=[
            pl.BlockSpec((tm, wc), lambda i: (i, 0)),
            pl.BlockSpec((1, wc), lambda i: (0, 0)),
        ],
        out_specs=pl.BlockSpec((TOP_K, tm), lambda i: (0, i)),
        out_shape=jax.ShapeDtypeStruct((TOP_K, n), jnp.int32),
        compiler_params=_params("parallel"),
        name="moe_slots",
    )(ei, first_row)


def _dispatch_kernel(dest_ref, last_ref, nv_ref, hn_ref, x_ref, zero_ref, zsem, tsem, rsem,
                     *, n_experts, n_blocks, tb, nw):
    i = pl.program_id(0)
    tm = hn_ref.shape[0] // nw
    n_tokens = pl.num_programs(0) * tm

    def row(ref, r):
        return ref.at[pl.ds(pl.multiple_of(r * nw, nw), nw), :]

    def zero_copy(block, sem):
        line0 = pl.multiple_of(jnp.maximum(block, 0) * (tb * nw), tb * nw)
        return pltpu.make_async_copy(zero_ref, x_ref.at[pl.ds(line0, tb * nw), :], sem)

    def for_expert_tails(fn):
        def body(e, c):
            @pl.when(last_ref[e] >= 0)
            def _():
                fn(zero_copy(last_ref[e], zsem.at[e]))
            return c
        lax.fori_loop(0, n_experts, body, 0)

    def for_unused_blocks(fn):
        def body(b, c):
            @pl.when(b >= nv_ref[0])
            def _():
                fn(zero_copy(b, tsem))
            return c
        lax.fori_loop(0, n_blocks, body, 0)

    @pl.when(i == 0)
    def _():
        zero_ref[...] = jnp.zeros_like(zero_ref)
        for_expert_tails(lambda cp: cp.start())
        for_unused_blocks(lambda cp: cp.start())
        for_expert_tails(lambda cp: cp.wait())
        for_unused_blocks(lambda cp: cp.wait())

    def start_one(j, c):
        for k in range(TOP_K):
            dst = dest_ref[k * n_tokens + i * tm + j]
            pltpu.make_async_copy(row(hn_ref, j), row(x_ref, dst), rsem).start()
        return c

    def wait_one(j, c):
        pltpu.make_async_copy(row(hn_ref, 0), row(x_ref, 0), rsem).wait()
        return c

    lax.fori_loop(0, tm, start_one, 0, unroll=4)
    lax.fori_loop(0, TOP_K * tm, wait_one, 0, unroll=8)


def _dispatch(hn_rows, dest, last_block, n_valid, n_blocks, nw, tm):
    n = hn_rows.shape[0] // nw
    tm = min(tm, n)
    n_experts = last_block.shape[0]
    tb = DISPATCH_BLOCK
    kern = functools.partial(_dispatch_kernel, n_experts=n_experts, n_blocks=n_blocks, tb=tb, nw=nw)
    grid_spec = pltpu.PrefetchScalarGridSpec(
        num_scalar_prefetch=3,
        grid=(n // tm,),
        in_specs=[pl.BlockSpec((tm * nw, LANES), lambda i, dest, last, nv: (i, 0))],
        out_specs=pl.BlockSpec(memory_space=pl.ANY),
        scratch_shapes=[
            pltpu.VMEM((tb * nw, LANES), jnp.uint32),
            pltpu.SemaphoreType.DMA((n_experts,)),
            pltpu.SemaphoreType.DMA,
            pltpu.SemaphoreType.DMA,
        ],
    )
    return pl.pallas_call(
        kern,
        grid_spec=grid_spec,
        out_shape=jax.ShapeDtypeStruct((n_blocks * tb * nw, LANES), jnp.uint32),
        compiler_params=_params("arbitrary"),
        name="moe_dispatch",
    )(dest, last_block, n_valid, hn_rows)


def _expert_kernel(*refs):
    la = WEIGHT_LOOKAHEAD
    be_ref, nv_ref, seg_ref = refs[:3]
    ahead = refs[3:3 + la]
    (x_ref, wg_hbm, wu_hbm, wd_hbm, o_ref, wg_buf, wu_buf, wd_buf, wgb_ref, wub_ref, wdb_ref, sem) = refs[3 + la:]
    i = pl.program_id(0)
    valid = i < nv_ref[0]
    new_expert = jnp.logical_or(i == 0, be_ref[i] != be_ref[jnp.maximum(i - 1, 0)])
    slot = seg_ref[i] % (la + 1)

    def weight_copies(e, s):
        return (pltpu.make_async_copy(wg_hbm.at[e], wg_buf.at[s], sem.at[s, 0]),
                pltpu.make_async_copy(wu_hbm.at[e], wu_buf.at[s], sem.at[s, 1]),
                pltpu.make_async_copy(wd_hbm.at[e], wd_buf.at[s], sem.at[s, 2]))

    def start_if_any(e, s):
        @pl.when(e >= 0)
        def _():
            for cp in weight_copies(e, s):
                cp.start()

    @pl.when(i == 0)
    def _():
        start_if_any(be_ref[0], 0)
        for k in range(la - 1):
            start_if_any(ahead[k][0], k + 1)

    @pl.when(jnp.logical_and(valid, new_expert))
    def _():
        start_if_any(ahead[la - 1][i], (seg_ref[i] + la) % (la + 1))
        for cp in weight_copies(be_ref[i], slot):
            cp.wait()
        wgb_ref[...] = wg_buf[slot].astype(BF16)
        wub_ref[...] = wu_buf[slot].astype(BF16)
        wdb_ref[...] = wd_buf[slot].astype(BF16)

    @pl.when(valid)
    def _():
        half = wgb_ref.shape[0] // 2
        nw = half // LANES
        x_lo, x_hi = _unpack_halves(_load_rows(x_ref, 0, x_ref.shape[0] // nw, nw))
        x_lo, x_hi = x_lo.astype(BF16), x_hi.astype(BF16)

        def proj(w_ref):
            return (jnp.dot(x_lo, w_ref[:half, :], preferred_element_type=F32)
                    + jnp.dot(x_hi, w_ref[half:, :], preferred_element_type=F32))

        hid = jax.nn.silu(proj(wgb_ref)) * proj(wub_ref)
        _store_rows(o_ref, _pack_halves(jnp.dot(hid.astype(BF16), wdb_ref[...], preferred_element_type=F32)))

    @pl.when(jnp.logical_not(valid))
    def _():
        o_ref[...] = jnp.zeros_like(o_ref)


def _experts(x_disp, block_expert, n_valid, block_segment, experts_ahead, w_gate, w_up, w_down):
    d, de = w_gate.shape[1], w_gate.shape[2]
    nw = d // (2 * LANES)
    rows = x_disp.shape[0] // nw
    tb = DISPATCH_BLOCK
    slots = WEIGHT_LOOKAHEAD + 1
    grid_spec = pltpu.PrefetchScalarGridSpec(
        num_scalar_prefetch=3 + WEIGHT_LOOKAHEAD,
        grid=(rows // tb,),
        in_specs=[
            pl.BlockSpec((tb * nw, LANES),
                         lambda i, be, nv, *_: (jnp.maximum(jnp.minimum(i, nv[0] - 1), 0), 0)),
            pl.BlockSpec(memory_space=pl.ANY),
            pl.BlockSpec(memory_space=pl.ANY),
            pl.BlockSpec(memory_space=pl.ANY),
        ],
        out_specs=pl.BlockSpec((tb * nw, LANES), lambda i, *_: (i, 0)),
        scratch_shapes=[
            pltpu.VMEM((slots, d, de), w_gate.dtype),
            pltpu.VMEM((slots, d, de), w_up.dtype),
            pltpu.VMEM((slots, de, d), w_down.dtype),
            pltpu.VMEM((d, de), BF16),
            pltpu.VMEM((d, de), BF16),
            pltpu.VMEM((de, d), BF16),
            pltpu.SemaphoreType.DMA((slots, 3)),
        ],
    )
    return pl.pallas_call(
        _expert_kernel,
        grid_spec=grid_spec,
        out_shape=jax.ShapeDtypeStruct((rows * nw, LANES), jnp.uint32),
        compiler_params=_params("arbitrary"),
        name="moe_experts",
    )(block_expert, n_valid, block_segment, *experts_ahead, x_disp, w_gate, w_up, w_down)


def _combined_halves(dest_ref, h_ref, ew_ref, y_ref, buf_ref, sem):
    tm, d = h_ref.shape
    nw = d // (2 * LANES)
    i = pl.program_id(0)
    n_tiles = pl.num_programs(0)

    def row(ref, r):
        return ref.at[pl.ds(pl.multiple_of(r * nw, nw), nw), :]

    def start_tile(tile, slot):
        def start_one(j, c):
            for k in range(TOP_K):
                src = dest_ref[k * (n_tiles * tm) + tile * tm + j]
                pltpu.make_async_copy(row(y_ref, src), row(buf_ref.at[slot], k * tm + j), sem.at[slot]).start()
            return c

        lax.fori_loop(0, tm, start_one, 0, unroll=4)

    def wait_tile(slot):
        def wait_one(j, c):
            pltpu.make_async_copy(row(y_ref, 0), row(buf_ref.at[slot], 0), sem.at[slot]).wait()
            return c

        lax.fori_loop(0, TOP_K * tm, wait_one, 0, unroll=8)

    slot = i % 2

    @pl.when(i == 0)
    def _():
        start_tile(0, 0)

    @pl.when(i + 1 < n_tiles)
    def _():
        start_tile(i + 1, 1 - slot)

    wait_tile(slot)
    h = h_ref[...]
    w = ew_ref[...]
    lo, hi = h[:, :d // 2], h[:, d // 2:]
    for k in range(TOP_K):
        y_lo, y_hi = _unpack_halves(_load_rows(buf_ref.at[slot], k * tm, tm, nw))
        lo = lo + y_lo * w[:, k:k + 1]
        hi = hi + y_hi * w[:, k:k + 1]
    return lo, hi


def _combine_kernel(dest_ref, h_ref, ew_ref, g_ref, y_ref, o_ref, buf_ref, sem, *, final_norm):
    d = h_ref.shape[1]
    lo, hi = _combined_halves(dest_ref, h_ref, ew_ref, y_ref, buf_ref, sem)
    if final_norm:
        ms = (jnp.sum(lo * lo, axis=-1, keepdims=True) + jnp.sum(hi * hi, axis=-1, keepdims=True)) / d
        scale = lax.rsqrt(ms + RMS_EPS)
        lo = lo * scale * g_ref[:, :d // 2]
        hi = hi * scale * g_ref[:, d // 2:]
    o_ref[:, :d // 2] = lo
    o_ref[:, d // 2:] = hi


def _combine(h2, y_disp, dest, ew, gain, final_norm, tm):
    n, d = h2.shape
    tm = min(tm, n)
    nw = d // (2 * LANES)
    wc = ew.shape[1]
    grid_spec = pltpu.PrefetchScalarGridSpec(
        num_scalar_prefetch=1,
        grid=(n // tm,),
        in_specs=[
            pl.BlockSpec((tm, d), lambda i, dest: (i, 0)),
            pl.BlockSpec((tm, wc), lambda i, dest: (i, 0)),
            pl.BlockSpec((1, d), lambda i, dest: (0, 0)),
            pl.BlockSpec(memory_space=pl.ANY),
        ],
        out_specs=pl.BlockSpec((tm, d), lambda i, dest: (i, 0)),
        scratch_shapes=[
            pltpu.VMEM((2, TOP_K * tm * nw, LANES), jnp.uint32),
            pltpu.SemaphoreType.DMA((2,)),
        ],
    )
    return pl.pallas_call(
        functools.partial(_combine_kernel, final_norm=final_norm),
        grid_spec=grid_spec,
        out_shape=jax.ShapeDtypeStruct((n, d), F32),
        compiler_params=_params("arbitrary"),
        name="moe_combine",
    )(dest, h2, ew, gain.reshape(1, d), y_disp)


def _hier_moe(h2, norm_gain, w_group, b_group, w_expert, b_expert, w_gate, w_up, w_down, final_gain):
    n, d = h2.shape
    n_groups = w_group.shape[1]
    n_experts = w_expert.shape[1]
    epg = n_experts // n_groups
    n_logits = n_groups + n_experts
    wc = -(-n_logits // LANES) * LANES
    w_router = jnp.concatenate(
        [w_group.astype(F32), w_expert.astype(F32), jnp.zeros((d, wc - n_logits), F32)], axis=1)
    b_router = jnp.concatenate(
        [b_group.astype(F32), b_expert.astype(F32), jnp.zeros((wc - n_logits,), F32)]).reshape(1, wc)
    hn_packed, ei, ew, counts = _router(h2, norm_gain, w_router, b_router, n_groups, epg, tm=1024, tp=512)

    tb = DISPATCH_BLOCK
    n_pairs = n * TOP_K
    counts = counts[0, :n_experts]
    blocks_per_expert = (counts + tb - 1) // tb
    block_end = jnp.cumsum(blocks_per_expert)
    block_start = block_end - blocks_per_expert
    first_row = jnp.pad((block_start * tb).astype(F32), (0, wc - n_experts)).reshape(1, wc)
    dest = _slots(ei, first_row, tm=2048).reshape(-1)
    last_block = jnp.where(blocks_per_expert > 0, block_end - 1, -1).astype(jnp.int32)
    n_blocks = -(-n_pairs // tb) + n_experts
    n_valid = block_end[-1:].astype(jnp.int32)
    blk = jnp.minimum(jnp.arange(n_blocks, dtype=jnp.int32), n_valid[0] - 1)
    block_expert = jnp.sum(block_end[None, :] <= blk[:, None], axis=1).astype(jnp.int32)
    expert_range = jnp.arange(n_experts, dtype=jnp.int32)
    owners = jnp.where(blocks_per_expert > 0, expert_range, n_experts)
    owner_at_or_after = lax.cummin(owners, axis=0, reverse=True)
    owner_after = jnp.concatenate([owner_at_or_after[1:], jnp.full((2,), n_experts, jnp.int32)])
    experts_ahead = []
    step = block_expert
    for _ in range(WEIGHT_LOOKAHEAD):
        step = jnp.take(owner_after, step)
        experts_ahead.append(jnp.where(step < n_experts, step, -1).astype(jnp.int32))
    changed = jnp.concatenate([jnp.zeros((1,), jnp.int32),
                               (block_expert[1:] != block_expert[:-1]).astype(jnp.int32)])
    block_segment = jnp.cumsum(changed).astype(jnp.int32)

    x_disp = _dispatch(hn_packed, dest, last_block, n_valid, n_blocks, d // (2 * LANES), tm=512)
    y_disp = _experts(x_disp, block_expert, n_valid, block_segment, experts_ahead, w_gate, w_up, w_down)
    gain = norm_gain if final_gain is None else final_gain
    return _combine(h2, y_disp, dest, ew, gain, final_gain is not None, tm=512)


def _forget_kernel(f_ref, b_ref, c_ref):
    heads = c_ref.shape[0]
    x = jax.nn.log_sigmoid(f_ref[...].T[:heads, :] + b_ref[...])
    s = x.shape[1]
    lane = lax.broadcasted_iota(jnp.int32, x.shape, 1)
    shift = 1
    while shift < s:
        x = x + jnp.where(lane >= shift, pltpu.roll(x, shift, axis=1), 0.0)
        shift *= 2
    c_ref[...] = x


def _forget_cumsum(f_logit, b_forget, b, s):
    h = b_forget.shape[0]
    return pl.pallas_call(
        _forget_kernel,
        grid=(b,),
        in_specs=[
            pl.BlockSpec((s, f_logit.shape[1]), lambda i: (i, 0)),
            pl.BlockSpec((h, 1), lambda i: (0, 0)),
        ],
        out_specs=pl.BlockSpec((None, h, s), lambda i: (i, 0, 0)),
        out_shape=jax.ShapeDtypeStruct((b, h, s), F32),
        compiler_params=_params("parallel"),
        name="fox_forget",
    )(f_logit, b_forget.astype(F32).reshape(h, 1))


def _fox_attn_kernel(q_ref, k_ref, v_ref, c_ref, o_ref, *, tk, n_sub):
    seq, dh = q_ref.shape
    head = pl.program_id(1)
    log2e = math.log2(math.e)
    row = lax.broadcasted_iota(jnp.int32, (tk, tk), 0)
    col = lax.broadcasted_iota(jnp.int32, (tk, tk), 1)
    causal = col <= row

    def attend(q, k, v, bias, carry, mask=None):
        m, l, acc = carry
        s = lax.dot_general(q, k, (((1,), (1,)), ((), ())), preferred_element_type=F32) + bias
        if mask is not None:
            s = jnp.where(mask, s, -jnp.inf)
        m_new = jnp.maximum(m, jnp.max(s, axis=-1, keepdims=True))
        alpha = jnp.exp2(m - m_new)
        p = jnp.exp2(s - m_new)
        l = alpha * l + jnp.sum(p, axis=-1, keepdims=True)
        acc = alpha * acc + jnp.dot(p.astype(BF16), v, preferred_element_type=F32)
        return m_new, l, acc

    def key_block(k0):
        bias = c_ref[pl.ds(head, 1), pl.ds(k0, tk)] * (-log2e)
        return k_ref[pl.ds(k0, tk), :], v_ref[pl.ds(k0, tk), :], bias

    for qt in range(seq // (tk * n_sub)):
        first = qt * n_sub
        qs = [q_ref[(first + a) * tk:(first + a + 1) * tk, :] for a in range(n_sub)]
        init = (jnp.full((tk, 1), -jnp.inf, F32), jnp.zeros((tk, 1), F32), jnp.zeros((tk, dh), F32))
        carries = (init,) * n_sub

        def kv_step(ki, carries, qs=qs):
            k, v, bias = key_block(pl.multiple_of(ki * tk, tk))
            return tuple(attend(qs[a], k, v, bias, carries[a]) for a in range(n_sub))

        if qt > 0:
            carries = lax.fori_loop(0, first, kv_step, carries)
        carries = list(carries)
        for kb in range(n_sub):
            k, v, bias = key_block((first + kb) * tk)
            for a in range(kb, n_sub):
                carries[a] = attend(qs[a], k, v, bias, carries[a], causal if a == kb else None)
        for a in range(n_sub):
            _, l, acc = carries[a]
            o_ref[(first + a) * tk:(first + a + 1) * tk, :] = (acc / l).astype(o_ref.dtype)


def _fox_attention(proj, cum, batch, seq, heads, tk, n_sub):
    dh = FOX_HEAD_DIM
    tk = min(tk, seq)
    n_sub = min(n_sub, seq // tk)
    kern = functools.partial(_fox_attn_kernel, tk=tk, n_sub=n_sub)
    return pl.pallas_call(
        kern,
        grid=(batch, heads),
        in_specs=[
            pl.BlockSpec((seq, dh), lambda b, h: (b, h)),
            pl.BlockSpec((seq, dh), lambda b, h: (b, heads + h)),
            pl.BlockSpec((seq, dh), lambda b, h: (b, 2 * heads + h)),
            pl.BlockSpec((None, heads, seq), lambda b, h: (b, 0, 0)),
        ],
        out_specs=pl.BlockSpec((seq, dh), lambda b, h: (b, h)),
        out_shape=jax.ShapeDtypeStruct((batch * seq, heads * dh), BF16),
        compiler_params=_params("parallel", "parallel"),
        name="fox_attention",
    )(proj, proj, proj, cum)


def _fox_out_kernel(o_ref, gate_ref, w_ref, h_ref, out_ref):
    gated = (o_ref[...].astype(F32) * jax.nn.sigmoid(gate_ref[...].astype(F32))).astype(BF16)
    out_ref[...] = h_ref[...] + jnp.dot(gated, w_ref[...], preferred_element_type=F32)


def _fox_out(o, proj, w_out, h2, tm):
    n, d = h2.shape
    tm = min(tm, n)
    gate_block = 3
    return pl.pallas_call(
        _fox_out_kernel,
        grid=(n // tm,),
        in_specs=[
            pl.BlockSpec((tm, d), lambda i: (i, 0)),
            pl.BlockSpec((tm, d), lambda i: (i, gate_block)),
            pl.BlockSpec((d, d), lambda i: (0, 0)),
            pl.BlockSpec((tm, d), lambda i: (i, 0)),
        ],
        out_specs=pl.BlockSpec((tm, d), lambda i: (i, 0)),
        out_shape=jax.ShapeDtypeStruct((n, d), F32),
        compiler_params=_params("parallel"),
        name="fox_out",
    )(o, proj, w_out, h2)


def _s5_layer(x, norm_gain, w_in, lam_re, lam_im, b_re, b_im, c_re, c_im, d_skip, log_step, w_glu, b_glu, w_out):
    b, s, d = x.shape
    assert b == STATE_BATCH
    ds = w_in.shape[1]
    bmat, cmat, a_re, a_im = _s5_discretise(lam_re, lam_im, b_re, b_im, c_re, c_im, log_step)
    u_tm = _s5_in(x, norm_gain, w_in.astype(BF16), ts=64)
    y_tm = _s5_scan(u_tm, bmat, cmat, a_re, a_im, d_skip.astype(F32), t_chunk=128)
    return _s5_out(y_tm, x, w_glu.astype(BF16), b_glu.astype(F32), w_out.astype(BF16), ts=64)


def _fox_layer(h, norm_gain, w_in, b_forget, w_out):
    b, s, d = h.shape
    heads = d // FOX_HEAD_DIM
    w_all = w_in.astype(BF16)
    w_forget = jnp.pad(w_in[:, 4 * d:], ((0, 0), (0, LANES - heads))).astype(BF16)
    q_scale = FOX_HEAD_DIM ** -0.5 * math.log2(math.e)
    col_scale = jnp.concatenate([jnp.full((d,), q_scale, F32), jnp.ones((3 * d,), F32)])
    proj, f_logit = _rms_matmul(h, norm_gain, w_all, 4 * d, col_scale, w_forget, BF16, ts=1024, tn=1024)
    cum = _forget_cumsum(f_logit, b_forget, b, s)
    o = _fox_attention(proj, cum, b, s, heads, tk=512, n_sub=4)
    return _fox_out(o, proj, w_out.astype(BF16), h.reshape(b * s, d), tm=512).reshape(b, s, d)


def kernel(x, l0_mix_norm, l0_s5_w_in, l0_s5_lambda_re, l0_s5_lambda_im, l0_s5_b_re, l0_s5_b_im, l0_s5_c_re, l0_s5_c_im, l0_s5_d, l0_s5_log_step, l0_s5_w_glu, l0_s5_b_glu, l0_s5_w_out, l0_ffn_norm, l0_moe_w_group, l0_moe_b_group, l0_moe_w_expert, l0_moe_b_expert, l0_moe_w_gate, l0_moe_w_up, l0_moe_w_down, l1_mix_norm, l1_fox_w_in, l1_fox_b_forget, l1_fox_w_out, l1_ffn_norm, l1_moe_w_group, l1_moe_b_group, l1_moe_w_expert, l1_moe_b_expert, l1_moe_w_gate, l1_moe_w_up, l1_moe_w_down, final_norm):
    b, s, d = x.shape
    h = _s5_layer(x, l0_mix_norm, l0_s5_w_in, l0_s5_lambda_re, l0_s5_lambda_im, l0_s5_b_re, l0_s5_b_im,
                  l0_s5_c_re, l0_s5_c_im, l0_s5_d, l0_s5_log_step, l0_s5_w_glu, l0_s5_b_glu, l0_s5_w_out)
    h = _hier_moe(h.reshape(b * s, d), l0_ffn_norm, l0_moe_w_group, l0_moe_b_group, l0_moe_w_expert,
                  l0_moe_b_expert, l0_moe_w_gate, l0_moe_w_up, l0_moe_w_down, None).reshape(b, s, d)
    h = _fox_layer(h, l1_mix_norm, l1_fox_w_in, l1_fox_b_forget, l1_fox_w_out)
    h = _hier_moe(h.reshape(b * s, d), l1_ffn_norm, l1_moe_w_group, l1_moe_b_group, l1_moe_w_expert,
                  l1_moe_b_expert, l1_moe_w_gate, l1_moe_w_up, l1_moe_w_down, final_norm)
    return h.reshape(b, s, d)
```

```python
import functools
import math

import jax
import jax.numpy as jnp
from jax import lax
from jax.experimental import pallas as pl
from jax.experimental.pallas import tpu as pltpu

F32 = jnp.float32
BF16 = jnp.bfloat16

RMS_EPS = 1e-6
STATE_BATCH = 8
SSM_GROUPS_PER_BLOCK = 8
FOX_HEAD_DIM = 128
TOP_K = 2
DISPATCH_BLOCK = 256
WEIGHT_LOOKAHEAD = 1
LANES = 128
V7X_VMEM_BYTES = 64 * 1024 * 1024
VMEM_LIMIT = V7X_VMEM_BYTES * 7 // 8

S5_TIME_TILE = 64
S5_SCAN_CHUNK = 128
ROUTER_BLOCK = 1024
ROUTER_PART = 512
SLOT_TILE = 2048
DISPATCH_TILE = 2048
TOKEN_TILE = 512
PROJ_ROWS = 1024
PROJ_COLS = 1024
ATTN_BLOCK = 512
ATTN_CHAINS = 4


def _params(*sem):
    return pltpu.CompilerParams(dimension_semantics=sem, vmem_limit_bytes=VMEM_LIMIT)


def _rms(x, gain):
    ms = jnp.mean(x * x, axis=-1, keepdims=True)
    return x * lax.rsqrt(ms + RMS_EPS) * gain


def _rms_matmul_kernel(x_ref, g_ref, w_ref, cs_ref, ws_ref, o_ref, side_ref, xn_ref):
    @pl.when(pl.program_id(2) == 0)
    def _():
        xn_ref[...] = _rms(x_ref[...], g_ref[...]).astype(BF16)
        side_ref[...] = jnp.dot(xn_ref[...], ws_ref[...], preferred_element_type=F32)

    acc = jnp.dot(xn_ref[...], w_ref[...], preferred_element_type=F32)
    o_ref[...] = (acc * cs_ref[...]).astype(o_ref.dtype)


def _rms_matmul(x, gain, w, n_cols, col_scale, w_side, out_dtype, ts, tn):
    b, s, d = x.shape
    m = n_cols
    ms = w_side.shape[1]
    ts, tn = min(ts, s), min(tn, m)
    ns = s // ts
    return pl.pallas_call(
        _rms_matmul_kernel,
        grid=(b, ns, m // tn),
        in_specs=[
            pl.BlockSpec((None, ts, d), lambda bi, si, j: (bi, si, 0)),
            pl.BlockSpec((1, d), lambda bi, si, j: (0, 0)),
            pl.BlockSpec((d, tn), lambda bi, si, j: (0, j)),
            pl.BlockSpec((1, tn), lambda bi, si, j: (0, j)),
            pl.BlockSpec((d, ms), lambda bi, si, j: (0, 0)),
        ],
        out_specs=[
            pl.BlockSpec((ts, tn), lambda bi, si, j: (bi * ns + si, j)),
            pl.BlockSpec((ts, ms), lambda bi, si, j: (bi * ns + si, 0)),
        ],
        out_shape=[jax.ShapeDtypeStruct((b * s, m), out_dtype), jax.ShapeDtypeStruct((b * s, ms), F32)],
        scratch_shapes=[pltpu.VMEM((ts, d), BF16)],
        compiler_params=_params("parallel", "parallel", "arbitrary"),
        name="rms_matmul",
    )(x, gain.reshape(1, d), w, col_scale.reshape(1, m), w_side)


def _s5_in_kernel(x_ref, g_ref, w_ref, u_ref):
    nb, ts, d = x_ref.shape
    nl = w_ref.shape[1] // LANES
    xn = _rms(x_ref[...].reshape(nb * ts, d), g_ref[...]).astype(BF16)
    u = jnp.dot(xn, w_ref[...], preferred_element_type=F32)
    for bi in range(nb):
        for c in range(nl):
            u_ref[pl.ds(bi * nl + c, ts, stride=nb * nl), :] = u[bi * ts:(bi + 1) * ts, c * LANES:(c + 1) * LANES]


def _s5_in(x, gain, w, ts):
    b, s, d = x.shape
    m = w.shape[1]
    nl = m // LANES
    ts = min(ts, s)
    return pl.pallas_call(
        _s5_in_kernel,
        grid=(s // ts,),
        in_specs=[
            pl.BlockSpec((b, ts, d), lambda si: (0, si, 0)),
            pl.BlockSpec((1, d), lambda si: (0, 0)),
            pl.BlockSpec((d, m), lambda si: (0, 0)),
        ],
        out_specs=pl.BlockSpec((ts * b * nl, LANES), lambda si: (si, 0)),
        out_shape=jax.ShapeDtypeStruct((s * b * nl, LANES), F32),
        compiler_params=_params("parallel"),
        name="s5_in",
    )(x, gain.reshape(1, d), w)


def _s5_scan_kernel(u_ref, bmat_ref, cmat_ref, are_ref, aim_ref, d_ref, y_ref, bu2_ref, st_ref, *, n_blocks, t_chunk):
    @pl.when(pl.program_id(0) == 0)
    def _():
        st_ref[...] = jnp.zeros_like(st_ref)

    cw = bmat_ref.shape[1]
    sw = bmat_ref.shape[2] // 2
    rows = t_chunk * STATE_BATCH
    for gb in range(n_blocks):
        u_blk = u_ref[pl.ds(gb, rows, stride=n_blocks), :]
        bu_ref = bu2_ref.at[gb % 2]
        bu_ref[...] = jnp.dot(u_blk.astype(BF16), bmat_ref[gb], preferred_element_type=F32)
        a_re = jnp.broadcast_to(are_ref[gb], (STATE_BATCH, sw))
        a_im = jnp.broadcast_to(aim_ref[gb], (STATE_BATCH, sw))

        def step(t, carry):
            s_re, s_im = carry
            r0 = pl.multiple_of(t * STATE_BATCH, STATE_BATCH)
            b_re = bu_ref[pl.ds(r0, STATE_BATCH), 0:sw]
            b_im = bu_ref[pl.ds(r0, STATE_BATCH), sw:2 * sw]
            n_re = a_re * s_re - a_im * s_im + b_re
            n_im = a_re * s_im + a_im * s_re + b_im
            bu_ref[pl.ds(r0, STATE_BATCH), 0:sw] = n_re
            bu_ref[pl.ds(r0, STATE_BATCH), sw:2 * sw] = n_im
            return n_re, n_im

        s_re, s_im = lax.fori_loop(0, t_chunk, step, (st_ref[gb, 0], st_ref[gb, 1]), unroll=True)
        st_ref[gb, 0] = s_re
        st_ref[gb, 1] = s_im
        y = jnp.dot(bu_ref[...].astype(BF16), cmat_ref[gb], preferred_element_type=F32)
        y_ref[pl.ds(gb, rows, stride=n_blocks), :] = y + d_ref[:, gb * cw:(gb + 1) * cw] * u_blk


def _s5_scan(u_tm, bmat, cmat, a_re, a_im, d_skip, t_chunk):
    nb, cw, sw2 = bmat.shape
    assert cw == LANES
    ds = nb * cw
    seq = u_tm.shape[0] // (STATE_BATCH * nb)
    t_chunk = min(t_chunk, seq)
    tr = t_chunk * STATE_BATCH
    kern = functools.partial(_s5_scan_kernel, n_blocks=nb, t_chunk=t_chunk)
    return pl.pallas_call(
        kern,
        grid=(seq // t_chunk,),
        in_specs=[
            pl.BlockSpec((tr * nb, LANES), lambda i: (i, 0)),
            pl.BlockSpec((nb, cw, sw2), lambda i: (0, 0, 0)),
            pl.BlockSpec((nb, sw2, cw), lambda i: (0, 0, 0)),
            pl.BlockSpec((nb, 1, sw2 // 2), lambda i: (0, 0, 0)),
            pl.BlockSpec((nb, 1, sw2 // 2), lambda i: (0, 0, 0)),
            pl.BlockSpec((1, ds), lambda i: (0, 0)),
        ],
        out_specs=pl.BlockSpec((tr * nb, LANES), lambda i: (i, 0)),
        out_shape=jax.ShapeDtypeStruct(u_tm.shape, F32),
        scratch_shapes=[
            pltpu.VMEM((2, tr, sw2), F32),
            pltpu.VMEM((nb, 2, STATE_BATCH, sw2 // 2), F32),
        ],
        compiler_params=_params("arbitrary"),
        name="s5_scan",
    )(u_tm, bmat, cmat, a_re, a_im, d_skip.reshape(1, ds))


def _s5_discretise(lam_re, lam_im, b_re, b_im, c_re, c_im, log_step):
    lr, li = lam_re.astype(F32), lam_im.astype(F32)
    step = jnp.exp(log_step.astype(F32))[:, None]
    mag = jnp.exp(lr * step)
    ab_re = mag * jnp.cos(li * step)
    ab_im = mag * jnp.sin(li * step)
    den = lr * lr + li * li
    nr, ni = ab_re - 1.0, ab_im
    fr = (nr * lr + ni * li) / den
    fi = (ni * lr - nr * li) / den
    br, bi = b_re.astype(F32), b_im.astype(F32)
    bb_re = fr[..., None] * br - fi[..., None] * bi
    bb_im = fr[..., None] * bi + fi[..., None] * br
    g, n, c = bb_re.shape
    gpb = min(SSM_GROUPS_PER_BLOCK, g)
    nb = g // gpb
    eye = jnp.eye(gpb, dtype=F32)

    def in_blocks(bb):
        return jnp.einsum("bjnc,jk->bjckn", bb.reshape(nb, gpb, n, c), eye).reshape(nb, gpb * c, gpb * n)

    def out_blocks(cc):
        return jnp.einsum("bjcn,jk->bjnkc", cc.reshape(nb, gpb, c, n), eye).reshape(nb, gpb * n, gpb * c)

    bmat = jnp.concatenate([in_blocks(bb_re), in_blocks(bb_im)], axis=-1).astype(BF16)
    cmat = jnp.concatenate([out_blocks(c_re.astype(F32)), -out_blocks(c_im.astype(F32))], axis=1).astype(BF16)
    return bmat, cmat, ab_re.reshape(nb, 1, gpb * n), ab_im.reshape(nb, 1, gpb * n)


def _s5_out_kernel(y_ref, x_ref, wg_ref, bg_ref, wo_ref, o_ref):
    nb, ts, d = x_ref.shape
    nl = wg_ref.shape[0] // LANES
    y = jnp.concatenate(
        [jnp.concatenate([y_ref[pl.ds(bi * nl + c, ts, stride=nb * nl), :] for c in range(nl)], axis=1)
         for bi in range(nb)], axis=0)
    g = jax.nn.gelu(y)
    z = jnp.dot(g.astype(BF16), wg_ref[...], preferred_element_type=F32) + bg_ref[...]
    gated = g * jax.nn.sigmoid(z)
    out = jnp.dot(gated.astype(BF16), wo_ref[...], preferred_element_type=F32)
    o_ref[...] = x_ref[...] + out.reshape(nb, ts, d)


def _s5_out(y_tm, x, w_glu, b_glu, w_out, ts):
    b, s, d = x.shape
    ds = w_glu.shape[0]
    nl = ds // LANES
    ts = min(ts, s)
    return pl.pallas_call(
        _s5_out_kernel,
        grid=(s // ts,),
        in_specs=[
            pl.BlockSpec((ts * b * nl, LANES), lambda si: (si, 0)),
            pl.BlockSpec((b, ts, d), lambda si: (0, si, 0)),
            pl.BlockSpec((ds, ds), lambda si: (0, 0)),
            pl.BlockSpec((1, ds), lambda si: (0, 0)),
            pl.BlockSpec((ds, d), lambda si: (0, 0)),
        ],
        out_specs=pl.BlockSpec((b, ts, d), lambda si: (0, si, 0)),
        out_shape=jax.ShapeDtypeStruct((b, s, d), F32),
        compiler_params=_params("parallel"),
        name="s5_out",
    )(y_tm, x, w_glu, b_glu.reshape(1, ds), w_out)


def _pack_halves(x):
    half = x.shape[-1] // 2
    lo = lax.bitcast_convert_type(x[:, :half].astype(BF16).astype(F32), jnp.uint32)
    hi = lax.bitcast_convert_type(x[:, half:].astype(BF16).astype(F32), jnp.uint32)
    return (lo >> 16) | (hi & jnp.uint32(0xFFFF0000))


def _unpack_halves(words):
    lo = lax.bitcast_convert_type(words << 16, F32)
    hi = lax.bitcast_convert_type(words & jnp.uint32(0xFFFF0000), F32)
    return lo, hi


def _store_rows(ref, words):
    rows, nw = words.shape[0], words.shape[1] // LANES
    for c in range(nw):
        ref[pl.ds(c, rows, stride=nw), :] = words[:, c * LANES:(c + 1) * LANES]


def _load_rows(ref, first_row, rows, nw):
    return jnp.concatenate(
        [ref[pl.ds(first_row * nw + c, rows, stride=nw), :] for c in range(nw)], axis=1)


def _router_kernel(h_ref, g_ref, whi_ref, wlo_ref, b_ref, tri_ref, hn_ref, ei_ref, ew_ref, cnt_ref, base_ref,
                   *, n_groups, epg):
    @pl.when(pl.program_id(0) == 0)
    def _():
        base_ref[...] = jnp.zeros_like(base_ref)

    tp = tri_ref.shape[0]
    nw = hn_ref.shape[0] // h_ref.shape[0]

    def route(first):
        hn = _rms(h_ref[first:first + tp, :], g_ref[...])
        _store_rows(hn_ref.at[first * nw:(first + tp) * nw, :], _pack_halves(hn))
        hn_hi = hn.astype(BF16)
        hn_lo = (hn - hn_hi.astype(F32)).astype(BF16)
        logits = (jnp.dot(hn_hi, whi_ref[...], preferred_element_type=F32)
                  + jnp.dot(hn_lo, whi_ref[...], preferred_element_type=F32)
                  + jnp.dot(hn_hi, wlo_ref[...], preferred_element_type=F32)) + b_ref[...]
        lane = lax.broadcasted_iota(jnp.int32, logits.shape, 1)
        n_lanes = logits.shape[1]

        def first_argmax(vals, vmax):
            first_lane = jnp.min(jnp.where(vals == vmax, lane, n_lanes).astype(F32), axis=-1, keepdims=True)
            return first_lane.astype(jnp.int32)

        in_groups = lane < n_groups
        gl = jnp.where(in_groups, logits, -jnp.inf)
        g_max = jnp.max(gl, axis=-1, keepdims=True)
        g_sel = first_argmax(gl, g_max)
        p_group = 1.0 / jnp.sum(jnp.where(in_groups, jnp.exp(logits - g_max), 0.0), axis=-1, keepdims=True)

        lo = n_groups + g_sel * epg
        el = jnp.where(jnp.logical_and(lane >= lo, lane < lo + epg), logits, -jnp.inf)
        t1 = jnp.max(el, axis=-1, keepdims=True)
        i1 = first_argmax(el, t1)
        el2 = jnp.where(lane == i1, -jnp.inf, el)
        t2 = jnp.max(el2, axis=-1, keepdims=True)
        i2 = first_argmax(el2, t2)
        e2x = jnp.exp(t2 - t1)
        w1 = p_group / (1.0 + e2x)
        w2 = p_group * e2x / (1.0 + e2x)
        e1 = i1 - n_groups
        e2 = i2 - n_groups

        onehot = jnp.logical_or(lane == e1, lane == e2)
        earlier = jnp.dot(tri_ref[...], jnp.where(onehot, 1.0, 0.0).astype(BF16), preferred_element_type=F32)
        total = earlier + base_ref[...]
        r1 = jnp.sum(jnp.where(lane == e1, total, 0.0), axis=-1, keepdims=True).astype(jnp.int32)
        r2 = jnp.sum(jnp.where(lane == e2, total, 0.0), axis=-1, keepdims=True).astype(jnp.int32)
        base_ref[...] = base_ref[...] + jnp.sum(jnp.where(onehot, 1.0, 0.0), axis=0, keepdims=True)

        ei_ref[first:first + tp, :] = jnp.where(
            lane == 0, e1, jnp.where(lane == 1, e2, jnp.where(lane == 2, r1, jnp.where(lane == 3, r2, 0))))
        ew_ref[first:first + tp, :] = jnp.where(lane == 0, w1, jnp.where(lane == 1, w2, 0.0))

    for part in range(h_ref.shape[0] // tp):
        route(part * tp)
    cnt_ref[...] = base_ref[...].astype(jnp.int32)


def _router(h2, gain, w_router, b_router, n_groups, epg, tm, tp):
    n, d = h2.shape
    tm = min(tm, n)
    tp = min(tp, tm)
    wc = w_router.shape[1]
    nw = d // (2 * LANES)
    w_hi = w_router.astype(BF16)
    w_lo = (w_router - w_hi.astype(F32)).astype(BF16)
    tri = jnp.tri(tp, k=-1, dtype=BF16)
    kern = functools.partial(_router_kernel, n_groups=n_groups, epg=epg)
    return pl.pallas_call(
        kern,
        grid=(n // tm,),
        in_specs=[
            pl.BlockSpec((tm, d), lambda i: (i, 0)),
            pl.BlockSpec((1, d), lambda i: (0, 0)),
            pl.BlockSpec((d, wc), lambda i: (0, 0)),
            pl.BlockSpec((d, wc), lambda i: (0, 0)),
            pl.BlockSpec((1, wc), lambda i: (0, 0)),
            pl.BlockSpec((tp, tp), lambda i: (0, 0)),
        ],
        out_specs=[
            pl.BlockSpec((tm * nw, LANES), lambda i: (i, 0)),
            pl.BlockSpec((tm, wc), lambda i: (i, 0)),
            pl.BlockSpec((tm, wc), lambda i: (i, 0)),
            pl.BlockSpec((1, wc), lambda i: (0, 0)),
        ],
        out_shape=[
            jax.ShapeDtypeStruct((n * nw, LANES), jnp.uint32),
            jax.ShapeDtypeStruct((n, wc), jnp.int32),
            jax.ShapeDtypeStruct((n, wc), F32),
            jax.ShapeDtypeStruct((1, wc), jnp.int32),
        ],
        scratch_shapes=[pltpu.VMEM((1, wc), F32)],
        compiler_params=_params("arbitrary"),
        name="moe_router",
    )(h2, gain.reshape(1, d), w_hi, w_lo, b_router, tri)


def _slot_kernel(ei_ref, first_ref, dest_ref):
    ei = ei_ref[...]
    lane = lax.broadcasted_iota(jnp.int32, ei.shape, 1)
    cols = jnp.zeros(ei.shape, F32)
    for k in range(TOP_K):
        first = jnp.sum(jnp.where(lane == ei[:, k:k + 1], first_ref[...], 0.0), axis=-1, keepdims=True)
        slot = first + ei[:, TOP_K + k:TOP_K + k + 1].astype(F32)
        cols = jnp.where(lane == k, slot, cols)
    dest_ref[...] = cols.T[:TOP_K, :].astype(jnp.int32)


def _slots(ei, first_row, tm):
    n, wc = ei.shape
    tm = min(tm, n)
    return pl.pallas_call(
        _slot_kernel,
        grid=(n // tm,),
        in_specs=[
            pl.BlockSpec((tm, wc), lambda i: (i, 0)),
            pl.BlockSpec((1, wc), lambda i: (0, 0)),
        ],
        out_specs=pl.BlockSpec((TOP_K, tm), lambda i: (0, i)),
        out_shape=jax.ShapeDtypeStruct((TOP_K, n), jnp.int32),
        compiler_params=_params("parallel"),
        name="moe_slots",
    )(ei, first_row)


def _dispatch_kernel(dest_ref, last_ref, nv_ref, hn_ref, x_ref, zero_ref, zsem, tsem, rsem,
                     *, n_experts, n_blocks, tb, nw):
    i = pl.program_id(0)
    tm = hn_ref.shape[0] // nw
    n_tokens = pl.num_programs(0) * tm

    def row(ref, r):
        return ref.at[pl.ds(pl.multiple_of(r * nw, nw), nw), :]

    def zero_copy(block, sem):
        line0 = pl.multiple_of(jnp.maximum(block, 0) * (tb * nw), tb * nw)
        return pltpu.make_async_copy(zero_ref, x_ref.at[pl.ds(line0, tb * nw), :], sem)

    def for_expert_tails(fn):
        def body(e, c):
            @pl.when(last_ref[e] >= 0)
            def _():
                fn(zero_copy(last_ref[e], zsem.at[e]))
            return c
        lax.fori_loop(0, n_experts, body, 0)

    def for_unused_blocks(fn):
        def body(b, c):
            @pl.when(b >= nv_ref[0])
            def _():
                fn(zero_copy(b, tsem))
            return c
        lax.fori_loop(0, n_blocks, body, 0)

    @pl.when(i == 0)
    def _():
        zero_ref[...] = jnp.zeros_like(zero_ref)
        for_expert_tails(lambda cp: cp.start())
        for_unused_blocks(lambda cp: cp.start())
        for_expert_tails(lambda cp: cp.wait())
        for_unused_blocks(lambda cp: cp.wait())

    def start_one(j, c):
        for k in range(TOP_K):
            dst = dest_ref[k * n_tokens + i * tm + j]
            pltpu.make_async_copy(row(hn_ref, j), row(x_ref, dst), rsem).start()
        return c

    def wait_one(j, c):
        pltpu.make_async_copy(row(hn_ref, 0), row(x_ref, 0), rsem).wait()
        return c

    lax.fori_loop(0, tm, start_one, 0, unroll=4)
    lax.fori_loop(0, TOP_K * tm, wait_one, 0, unroll=8)


def _dispatch(hn_rows, dest, last_block, n_valid, n_blocks, nw, tm):
    n = hn_rows.shape[0] // nw
    tm = min(tm, n)
    n_experts = last_block.shape[0]
    tb = DISPATCH_BLOCK
    kern = functools.partial(_dispatch_kernel, n_experts=n_experts, n_blocks=n_blocks, tb=tb, nw=nw)
    grid_spec = pltpu.PrefetchScalarGridSpec(
        num_scalar_prefetch=3,
        grid=(n // tm,),
        in_specs=[pl.BlockSpec((tm * nw, LANES), lambda i, dest, last, nv: (i, 0))],
        out_specs=pl.BlockSpec(memory_space=pl.ANY),
        scratch_shapes=[
            pltpu.VMEM((tb * nw, LANES), jnp.uint32),
            pltpu.SemaphoreType.DMA((n_experts,)),
            pltpu.SemaphoreType.DMA,
            pltpu.SemaphoreType.DMA,
        ],
    )
    return pl.pallas_call(
        kern,
        grid_spec=grid_spec,
        out_shape=jax.ShapeDtypeStruct((n_blocks * tb * nw, LANES), jnp.uint32),
        compiler_params=_params("arbitrary"),
        name="moe_dispatch",
    )(dest, last_block, n_valid, hn_rows)


def _expert_kernel(*refs):
    la = WEIGHT_LOOKAHEAD
    be_ref, nv_ref, seg_ref = refs[:3]
    ahead = refs[3:3 + la]
    (x_ref, wg_hbm, wu_hbm, wd_hbm, o_ref, wg_buf, wu_buf, wd_buf, wgb_ref, wub_ref, wdb_ref, sem) = refs[3 + la:]
    i = pl.program_id(0)
    valid = i < nv_ref[0]
    new_expert = jnp.logical_or(i == 0, be_ref[i] != be_ref[jnp.maximum(i - 1, 0)])
    slot = seg_ref[i] % (la + 1)

    def weight_copies(e, s):
        return (pltpu.make_async_copy(wg_hbm.at[e], wg_buf.at[s], sem.at[s, 0]),
                pltpu.make_async_copy(wu_hbm.at[e], wu_buf.at[s], sem.at[s, 1]),
                pltpu.make_async_copy(wd_hbm.at[e], wd_buf.at[s], sem.at[s, 2]))

    def start_if_any(e, s):
        @pl.when(e >= 0)
        def _():
            for cp in weight_copies(e, s):
                cp.start()

    @pl.when(i == 0)
    def _():
        start_if_any(be_ref[0], 0)
        for k in range(la - 1):
            start_if_any(ahead[k][0], k + 1)

    @pl.when(jnp.logical_and(valid, new_expert))
    def _():
        start_if_any(ahead[la - 1][i], (seg_ref[i] + la) % (la + 1))
        for cp in weight_copies(be_ref[i], slot):
            cp.wait()
        wgb_ref[...] = wg_buf[slot].astype(BF16)
        wub_ref[...] = wu_buf[slot].astype(BF16)
        wdb_ref[...] = wd_buf[slot].astype(BF16)

    @pl.when(valid)
    def _():
        half = wgb_ref.shape[0] // 2
        nw = half // LANES
        x_lo, x_hi = _unpack_halves(_load_rows(x_ref, 0, x_ref.shape[0] // nw, nw))
        x_lo, x_hi = x_lo.astype(BF16), x_hi.astype(BF16)

        def proj(w_ref):
            return (jnp.dot(x_lo, w_ref[:half, :], preferred_element_type=F32)
                    + jnp.dot(x_hi, w_ref[half:, :], preferred_element_type=F32))

        hid = jax.nn.silu(proj(wgb_ref)) * proj(wub_ref)
        _store_rows(o_ref, _pack_halves(jnp.dot(hid.astype(BF16), wdb_ref[...], preferred_element_type=F32)))

    @pl.when(jnp.logical_not(valid))
    def _():
        o_ref[...] = jnp.zeros_like(o_ref)


def _experts(x_disp, block_expert, n_valid, block_segment, experts_ahead, w_gate, w_up, w_down):
    d, de = w_gate.shape[1], w_gate.shape[2]
    nw = d // (2 * LANES)
    rows = x_disp.shape[0] // nw
    tb = DISPATCH_BLOCK
    slots = WEIGHT_LOOKAHEAD + 1
    grid_spec = pltpu.PrefetchScalarGridSpec(
        num_scalar_prefetch=3 + WEIGHT_LOOKAHEAD,
        grid=(rows // tb,),
        in_specs=[
            pl.BlockSpec((tb * nw, LANES),
                         lambda i, be, nv, *_: (jnp.maximum(jnp.minimum(i, nv[0] - 1), 0), 0)),
            pl.BlockSpec(memory_space=pl.ANY),
            pl.BlockSpec(memory_space=pl.ANY),
            pl.BlockSpec(memory_space=pl.ANY),
        ],
        out_specs=pl.BlockSpec((tb * nw, LANES), lambda i, *_: (i, 0)),
        scratch_shapes=[
            pltpu.VMEM((slots, d, de), w_gate.dtype),
            pltpu.VMEM((slots, d, de), w_up.dtype),
            pltpu.VMEM((slots, de, d), w_down.dtype),
            pltpu.VMEM((d, de), BF16),
            pltpu.VMEM((d, de), BF16),
            pltpu.VMEM((de, d), BF16),
            pltpu.SemaphoreType.DMA((slots, 3)),
        ],
    )
    return pl.pallas_call(
        _expert_kernel,
        grid_spec=grid_spec,
        out_shape=jax.ShapeDtypeStruct((rows * nw, LANES), jnp.uint32),
        compiler_params=_params("arbitrary"),
        name="moe_experts",
    )(block_expert, n_valid, block_segment, *experts_ahead, x_disp, w_gate, w_up, w_down)


def _combined_halves(dest_ref, h_ref, ew_ref, y_ref, buf_ref, sem):
    tm, d = h_ref.shape
    nw = d // (2 * LANES)
    i = pl.program_id(0)
    n_tiles = pl.num_programs(0)

    def row(ref, r):
        return ref.at[pl.ds(pl.multiple_of(r * nw, nw), nw), :]

    def start_tile(tile, slot):
        def start_one(j, c):
            for k in range(TOP_K):
                src = dest_ref[k * (n_tiles * tm) + tile * tm + j]
                pltpu.make_async_copy(row(y_ref, src), row(buf_ref.at[slot], k * tm + j), sem.at[slot]).start()
            return c

        lax.fori_loop(0, tm, start_one, 0, unroll=4)

    def wait_tile(slot):
        def wait_one(j, c):
            pltpu.make_async_copy(row(y_ref, 0), row(buf_ref.at[slot], 0), sem.at[slot]).wait()
            return c

        lax.fori_loop(0, TOP_K * tm, wait_one, 0, unroll=8)

    slot = i % 2

    @pl.when(i == 0)
    def _():
        start_tile(0, 0)

    @pl.when(i + 1 < n_tiles)
    def _():
        start_tile(i + 1, 1 - slot)

    wait_tile(slot)
    h = h_ref[...]
    w = ew_ref[...]
    lo, hi = h[:, :d // 2], h[:, d // 2:]
    for k in range(TOP_K):
        y_lo, y_hi = _unpack_halves(_load_rows(buf_ref.at[slot], k * tm, tm, nw))
        lo = lo + y_lo * w[:, k:k + 1]
        hi = hi + y_hi * w[:, k:k + 1]
    return lo, hi


def _combine_kernel(dest_ref, h_ref, ew_ref, g_ref, y_ref, o_ref, buf_ref, sem, *, final_norm):
    d = h_ref.shape[1]
    lo, hi = _combined_halves(dest_ref, h_ref, ew_ref, y_ref, buf_ref, sem)
    if final_norm:
        ms = (jnp.sum(lo * lo, axis=-1, keepdims=True) + jnp.sum(hi * hi, axis=-1, keepdims=True)) / d
        scale = lax.rsqrt(ms + RMS_EPS)
        lo = lo * scale * g_ref[:, :d // 2]
        hi = hi * scale * g_ref[:, d // 2:]
    o_ref[:, :d // 2] = lo
    o_ref[:, d // 2:] = hi


def _combine(h2, y_disp, dest, ew, gain, final_norm, tm):
    n, d = h2.shape
    tm = min(tm, n)
    nw = d // (2 * LANES)
    wc = ew.shape[1]
    grid_spec = pltpu.PrefetchScalarGridSpec(
        num_scalar_prefetch=1,
        grid=(n // tm,),
        in_specs=[
            pl.BlockSpec((tm, d), lambda i, dest: (i, 0)),
            pl.BlockSpec((tm, wc), lambda i, dest: (i, 0)),
            pl.BlockSpec((1, d), lambda i, dest: (0, 0)),
            pl.BlockSpec(memory_space=pl.ANY),
        ],
        out_specs=pl.BlockSpec((tm, d), lambda i, dest: (i, 0)),
        scratch_shapes=[
            pltpu.VMEM((2, TOP_K * tm * nw, LANES), jnp.uint32),
            pltpu.SemaphoreType.DMA((2,)),
        ],
    )
    return pl.pallas_call(
        functools.partial(_combine_kernel, final_norm=final_norm),
        grid_spec=grid_spec,
        out_shape=jax.ShapeDtypeStruct((n, d), F32),
        compiler_params=_params("arbitrary"),
        name="moe_combine",
    )(dest, h2, ew, gain.reshape(1, d), y_disp)


def _hier_moe(h2, norm_gain, w_group, b_group, w_expert, b_expert, w_gate, w_up, w_down, final_gain):
    n, d = h2.shape
    n_groups = w_group.shape[1]
    n_experts = w_expert.shape[1]
    epg = n_experts // n_groups
    n_logits = n_groups + n_experts
    wc = -(-n_logits // LANES) * LANES
    w_router = jnp.concatenate(
        [w_group.astype(F32), w_expert.astype(F32), jnp.zeros((d, wc - n_logits), F32)], axis=1)
    b_router = jnp.concatenate(
        [b_group.astype(F32), b_expert.astype(F32), jnp.zeros((wc - n_logits,), F32)]).reshape(1, wc)
    hn_packed, ei, ew, counts = _router(h2, norm_gain, w_router, b_router, n_groups, epg,
                                        tm=ROUTER_BLOCK, tp=ROUTER_PART)

    tb = DISPATCH_BLOCK
    n_pairs = n * TOP_K
    counts = counts[0, :n_experts]
    blocks_per_expert = (counts + tb - 1) // tb
    block_end = jnp.cumsum(blocks_per_expert)
    block_start = block_end - blocks_per_expert
    first_row = jnp.pad((block_start * tb).astype(F32), (0, wc - n_experts)).reshape(1, wc)
    dest = _slots(ei, first_row, tm=SLOT_TILE).reshape(-1)
    last_block = jnp.where(blocks_per_expert > 0, block_end - 1, -1).astype(jnp.int32)
    n_blocks = -(-n_pairs // tb) + n_experts
    n_valid = block_end[-1:].astype(jnp.int32)
    blk = jnp.minimum(jnp.arange(n_blocks, dtype=jnp.int32), n_valid[0] - 1)
    block_expert = jnp.sum(block_end[None, :] <= blk[:, None], axis=1).astype(jnp.int32)
    expert_range = jnp.arange(n_experts, dtype=jnp.int32)
    owners = jnp.where(blocks_per_expert > 0, expert_range, n_experts)
    owner_at_or_after = lax.cummin(owners, axis=0, reverse=True)
    owner_after = jnp.concatenate([owner_at_or_after[1:], jnp.full((2,), n_experts, jnp.int32)])
    experts_ahead = []
    step = block_expert
    for _ in range(WEIGHT_LOOKAHEAD):
        step = jnp.take(owner_after, step)
        experts_ahead.append(jnp.where(step < n_experts, step, -1).astype(jnp.int32))
    changed = jnp.concatenate([jnp.zeros((1,), jnp.int32),
                               (block_expert[1:] != block_expert[:-1]).astype(jnp.int32)])
    block_segment = jnp.cumsum(changed).astype(jnp.int32)

    x_disp = _dispatch(hn_packed, dest, last_block, n_valid, n_blocks, d // (2 * LANES), tm=DISPATCH_TILE)
    y_disp = _experts(x_disp, block_expert, n_valid, block_segment, experts_ahead, w_gate, w_up, w_down)
    gain = norm_gain if final_gain is None else final_gain
    return _combine(h2, y_disp, dest, ew, gain, final_gain is not None, tm=TOKEN_TILE)


def _forget_kernel(f_ref, b_ref, c_ref):
    heads = c_ref.shape[0]
    x = jax.nn.log_sigmoid(f_ref[...].T[:heads, :] + b_ref[...])
    s = x.shape[1]
    lane = lax.broadcasted_iota(jnp.int32, x.shape, 1)
    shift = 1
    while shift < s:
        x = x + jnp.where(lane >= shift, pltpu.roll(x, shift, axis=1), 0.0)
        shift *= 2
    c_ref[...] = x


def _forget_cumsum(f_logit, b_forget, b, s):
    h = b_forget.shape[0]
    return pl.pallas_call(
        _forget_kernel,
        grid=(b,),
        in_specs=[
            pl.BlockSpec((s, f_logit.shape[1]), lambda i: (i, 0)),
            pl.BlockSpec((h, 1), lambda i: (0, 0)),
        ],
        out_specs=pl.BlockSpec((None, h, s), lambda i: (i, 0, 0)),
        out_shape=jax.ShapeDtypeStruct((b, h, s), F32),
        compiler_params=_params("parallel"),
        name="fox_forget",
    )(f_logit, b_forget.astype(F32).reshape(h, 1))


def _fox_attn_kernel(q_ref, k_ref, v_ref, c_ref, o_ref, *, tk, n_sub):
    seq, dh = q_ref.shape
    head = pl.program_id(1)
    log2e = math.log2(math.e)
    row = lax.broadcasted_iota(jnp.int32, (tk, tk), 0)
    col = lax.broadcasted_iota(jnp.int32, (tk, tk), 1)
    causal = col <= row

    def attend(q, k, v, bias, carry, mask=None):
        m, l, acc = carry
        s = lax.dot_general(q, k, (((1,), (1,)), ((), ())), preferred_element_type=F32) + bias
        if mask is not None:
            s = jnp.where(mask, s, -jnp.inf)
        m_new = jnp.maximum(m, jnp.max(s, axis=-1, keepdims=True))
        alpha = jnp.exp2(m - m_new)
        p = jnp.exp2(s - m_new)
        l = alpha * l + jnp.sum(p, axis=-1, keepdims=True)
        acc = alpha * acc + jnp.dot(p.astype(BF16), v, preferred_element_type=F32)
        return m_new, l, acc

    def key_block(k0):
        bias = c_ref[pl.ds(head, 1), pl.ds(k0, tk)] * (-log2e)
        return k_ref[pl.ds(k0, tk), :], v_ref[pl.ds(k0, tk), :], bias

    for qt in range(seq // (tk * n_sub)):
        first = qt * n_sub
        qs = [q_ref[(first + a) * tk:(first + a + 1) * tk, :] for a in range(n_sub)]
        init = (jnp.full((tk, 1), -jnp.inf, F32), jnp.zeros((tk, 1), F32), jnp.zeros((tk, dh), F32))
        carries = (init,) * n_sub

        def kv_step(ki, carries, qs=qs):
            k, v, bias = key_block(pl.multiple_of(ki * tk, tk))
            return tuple(attend(qs[a], k, v, bias, carries[a]) for a in range(n_sub))

        if qt > 0:
            carries = lax.fori_loop(0, first, kv_step, carries)
        carries = list(carries)
        for kb in range(n_sub):
            k, v, bias = key_block((first + kb) * tk)
            for a in range(kb, n_sub):
                carries[a] = attend(qs[a], k, v, bias, carries[a], causal if a == kb else None)
        for a in range(n_sub):
            _, l, acc = carries[a]
            o_ref[(first + a) * tk:(first + a + 1) * tk, :] = (acc / l).astype(o_ref.dtype)


def _fox_attention(proj, cum, batch, seq, heads, tk, n_sub):
    dh = FOX_HEAD_DIM
    tk = min(tk, seq)
    n_sub = min(n_sub, seq // tk)
    kern = functools.partial(_fox_attn_kernel, tk=tk, n_sub=n_sub)
    return pl.pallas_call(
        kern,
        grid=(batch, heads),
        in_specs=[
            pl.BlockSpec((seq, dh), lambda b, h: (b, h)),
            pl.BlockSpec((seq, dh), lambda b, h: (b, heads + h)),
            pl.BlockSpec((seq, dh), lambda b, h: (b, 2 * heads + h)),
            pl.BlockSpec((None, heads, seq), lambda b, h: (b, 0, 0)),
        ],
        out_specs=pl.BlockSpec((seq, dh), lambda b, h: (b, h)),
        out_shape=jax.ShapeDtypeStruct((batch * seq, heads * dh), BF16),
        compiler_params=_params("parallel", "parallel"),
        name="fox_attention",
    )(proj, proj, proj, cum)


def _fox_out_kernel(o_ref, gate_ref, w_ref, h_ref, out_ref):
    gated = (o_ref[...].astype(F32) * jax.nn.sigmoid(gate_ref[...].astype(F32))).astype(BF16)
    out_ref[...] = h_ref[...] + jnp.dot(gated, w_ref[...], preferred_element_type=F32)


def _fox_out(o, proj, w_out, h2, tm):
    n, d = h2.shape
    tm = min(tm, n)
    gate_block = 3
    return pl.pallas_call(
        _fox_out_kernel,
        grid=(n // tm,),
        in_specs=[
            pl.BlockSpec((tm, d), lambda i: (i, 0)),
            pl.BlockSpec((tm, d), lambda i: (i, gate_block)),
            pl.BlockSpec((d, d), lambda i: (0, 0)),
            pl.BlockSpec((tm, d), lambda i: (i, 0)),
        ],
        out_specs=pl.BlockSpec((tm, d), lambda i: (i, 0)),
        out_shape=jax.ShapeDtypeStruct((n, d), F32),
        compiler_params=_params("parallel"),
        name="fox_out",
    )(o, proj, w_out, h2)


def _s5_layer(x, norm_gain, w_in, lam_re, lam_im, b_re, b_im, c_re, c_im, d_skip, log_step, w_glu, b_glu, w_out):
    b, s, d = x.shape
    assert b == STATE_BATCH
    ds = w_in.shape[1]
    bmat, cmat, a_re, a_im = _s5_discretise(lam_re, lam_im, b_re, b_im, c_re, c_im, log_step)
    u_tm = _s5_in(x, norm_gain, w_in.astype(BF16), ts=S5_TIME_TILE)
    y_tm = _s5_scan(u_tm, bmat, cmat, a_re, a_im, d_skip.astype(F32), t_chunk=S5_SCAN_CHUNK)
    return _s5_out(y_tm, x, w_glu.astype(BF16), b_glu.astype(F32), w_out.astype(BF16), ts=S5_TIME_TILE)


def _fox_layer(h, norm_gain, w_in, b_forget, w_out):
    b, s, d = h.shape
    heads = d // FOX_HEAD_DIM
    w_all = w_in.astype(BF16)
    w_forget = jnp.pad(w_in[:, 4 * d:], ((0, 0), (0, LANES - heads))).astype(BF16)
    q_scale = FOX_HEAD_DIM ** -0.5 * math.log2(math.e)
    col_scale = jnp.concatenate([jnp.full((d,), q_scale, F32), jnp.ones((3 * d,), F32)])
    proj, f_logit = _rms_matmul(h, norm_gain, w_all, 4 * d, col_scale, w_forget, BF16, ts=PROJ_ROWS, tn=PROJ_COLS)
    cum = _forget_cumsum(f_logit, b_forget, b, s)
    o = _fox_attention(proj, cum, b, s, heads, tk=ATTN_BLOCK, n_sub=ATTN_CHAINS)
    return _fox_out(o, proj, w_out.astype(BF16), h.reshape(b * s, d), tm=TOKEN_TILE).reshape(b, s, d)


def kernel(x, l0_mix_norm, l0_s5_w_in, l0_s5_lambda_re, l0_s5_lambda_im, l0_s5_b_re, l0_s5_b_im, l0_s5_c_re, l0_s5_c_im, l0_s5_d, l0_s5_log_step, l0_s5_w_glu, l0_s5_b_glu, l0_s5_w_out, l0_ffn_norm, l0_moe_w_group, l0_moe_b_group, l0_moe_w_expert, l0_moe_b_expert, l0_moe_w_gate, l0_moe_w_up, l0_moe_w_down, l1_mix_norm, l1_fox_w_in, l1_fox_b_forget, l1_fox_w_out, l1_ffn_norm, l1_moe_w_group, l1_moe_b_group, l1_moe_w_expert, l1_moe_b_expert, l1_moe_w_gate, l1_moe_w_up, l1_moe_w_down, final_norm):
    b, s, d = x.shape
    h = _s5_layer(x, l0_mix_norm, l0_s5_w_in, l0_s5_lambda_re, l0_s5_lambda_im, l0_s5_b_re, l0_s5_b_im,
                  l0_s5_c_re, l0_s5_c_im, l0_s5_d, l0_s5_log_step, l0_s5_w_glu, l0_s5_b_glu, l0_s5_w_out)
    h = _hier_moe(h.reshape(b * s, d), l0_ffn_norm, l0_moe_w_group, l0_moe_b_group, l0_moe_w_expert,
                  l0_moe_b_expert, l0_moe_w_gate, l0_moe_w_up, l0_moe_w_down, None).reshape(b, s, d)
    h = _fox_layer(h, l1_mix_norm, l1_fox_w_in, l1_fox_b_forget, l1_fox_w_out)
    h = _hier_moe(h.reshape(b * s, d), l1_ffn_norm, l1_moe_w_group, l1_moe_b_group, l1_moe_w_expert,
                  l1_moe_b_expert, l1_moe_w_gate, l1_moe_w_up, l1_moe_w_down, final_norm)
    return h.reshape(b, s, d)
```
